```python
import jax
import jax.numpy as jnp
from jax import lax
import numpy as np

D_MODEL = 1024
BATCH = 4
SEQ = 4096
DEPTH = 2
DEC_BATCH = 32
DEC_SEQ = 1
PAST_LEN = 16384
PAGE_SIZE = 128

N_EVEN = (DEPTH + 1) // 2
N_ODD = DEPTH // 2
MIX_W = D_MODEL
A_W = MIX_W // 2
B_W = MIX_W - A_W
H_B = 8
HD_B = B_W // H_B
CONV_A_W = 3
Q_BLOCK = 128
E_IN = 3 * A_W + 3 * B_W + H_B
C_W = MIX_W // 2
H_C = 4
DK_C = C_W // H_C
CHUNK = 128
P_W = MIX_W - C_W
POOL_WINDOWS = (2, 4, 8, 16)
POOL_G = P_W // len(POOL_WINDOWS)
POOL_PREV = max(POOL_WINDOWS) - 1
O_IN = 4 * C_W + 2 * H_C + P_W
D_FF = ((8 * D_MODEL // 3 + 127) // 128) * 128
FFN_CONV_W = 3
EPS = 1e-6

kernel_name = 'hybrid_conv_fox_mlstm_pool_step'


def rms_norm(x, g):
    xf = x.astype(jnp.float32)
    y = xf * lax.rsqrt(jnp.mean(xf * xf, axis=-1, keepdims=True) + EPS)
    return (y * g.astype(jnp.float32)).astype(x.dtype)


def causal_dwconv(u, prev, w):
    width = w.shape[0]
    t_len = u.shape[1]
    ext = jnp.concatenate([prev.astype(u.dtype), u], axis=1)
    y = w[0] * ext[:, 0:t_len]
    for j in range(1, width):
        y = y + w[j] * ext[:, j:j + t_len]
    return y, ext[:, ext.shape[1] - (width - 1):]


def gather_pages(cache, page_table):
    g = cache[page_table]
    return g.reshape((g.shape[0], g.shape[1] * g.shape[2]) + g.shape[3:])


def even_project(h, w_in, b_f, conv_w, conv_prev):
    z = h @ w_in
    cuts = np.cumsum([A_W, A_W, A_W, B_W, B_W, B_W]).tolist()
    u, gb, gc, q, k, v, fz = jnp.split(z, cuts, axis=-1)
    a, conv_new = causal_dwconv(gc * u, conv_prev, conv_w)
    a = gb * a
    logf = jax.nn.log_sigmoid(fz.astype(jnp.float32) + b_f.astype(jnp.float32))
    shp = h.shape[:2] + (H_B, HD_B)
    return a, q.reshape(shp), k.reshape(shp), v.reshape(shp), logf, conv_new


def fox_prompt(q, k, v, logf):
    b, s = q.shape[:2]
    nb = s // Q_BLOCK
    f_cum = jnp.cumsum(logf, axis=1).transpose(0, 2, 1)
    qb = q.reshape(b, nb, Q_BLOCK, H_B, HD_B).transpose(1, 0, 2, 3, 4)
    fb = f_cum.reshape(b, H_B, nb, Q_BLOCK).transpose(2, 0, 1, 3)
    k_pos = jnp.arange(s)
    scale = HD_B ** -0.5

    def block(args):
        i, q_i, f_i = args
        sc = jnp.einsum('bqhd,bkhd->bhqk', q_i, k).astype(jnp.float32) * scale
        sc = sc + (f_i[..., :, None] - f_cum[..., None, :])
        q_pos = i * Q_BLOCK + jnp.arange(Q_BLOCK)
        sc = jnp.where(k_pos[None, :] <= q_pos[:, None], sc, -jnp.inf)
        p = jax.nn.softmax(sc, axis=-1).astype(v.dtype)
        return jnp.einsum('bhqk,bkhd->bqhd', p, v)

    o = lax.map(block, (jnp.arange(nb), qb, fb))
    return o.transpose(1, 0, 2, 3, 4).reshape(b, s, H_B * HD_B)


def fox_sample(q, k, v, logf, k_past, v_past, logf_past):
    p_len, t_len = k_past.shape[1], q.shape[1]
    f_cum = jnp.cumsum(jnp.concatenate([logf_past.astype(jnp.float32), logf], axis=1), axis=1).transpose(0, 2, 1)
    f_q = f_cum[..., p_len:]
    scale = HD_B ** -0.5
    s_past = (jnp.einsum('bqhd,bkhd->bhqk', q, k_past.astype(q.dtype)).astype(jnp.float32) * scale
              + (f_q[..., :, None] - f_cum[..., None, :p_len]))
    s_new = (jnp.einsum('bqhd,bkhd->bhqk', q, k).astype(jnp.float32) * scale
             + (f_q[..., :, None] - f_q[..., None, :]))
    causal = jnp.tril(jnp.ones((t_len, t_len), dtype=bool))
    s_new = jnp.where(causal, s_new, -jnp.inf)
    p = jax.nn.softmax(jnp.concatenate([s_past, s_new], axis=-1), axis=-1).astype(v.dtype)
    o = (jnp.einsum('bhqk,bkhd->bqhd', p[..., :p_len], v_past.astype(v.dtype))
         + jnp.einsum('bhqk,bkhd->bqhd', p[..., p_len:], v))
    return o.reshape(q.shape[0], t_len, H_B * HD_B)


def odd_project(h, w_in, b_i, b_f):
    z = h @ w_in
    cuts = np.cumsum([C_W, C_W, C_W, C_W, H_C, H_C]).tolist()
    q, k, v, o, iz, fz, p = jnp.split(z, cuts, axis=-1)
    shp = h.shape[:2] + (H_C, DK_C)
    log_i = iz.astype(jnp.float32) + b_i.astype(jnp.float32)
    log_f = jax.nn.log_sigmoid(fz.astype(jnp.float32) + b_f.astype(jnp.float32))
    return (q.reshape(shp), (k * DK_C ** -0.5).reshape(shp), v.reshape(shp),
            jax.nn.sigmoid(o), log_i, log_f, p)


def mlstm_chunk(carry, inp):
    c, n, m = carry
    q, k, v, li, lf = inp
    l_len = q.shape[2]
    bcum = jnp.cumsum(lf, axis=-1)
    causal = jnp.tril(jnp.ones((l_len, l_len), dtype=bool))
    dlog = jnp.where(causal, bcum[..., :, None] - bcum[..., None, :] + li[..., None, :], -jnp.inf)
    inter = bcum + m[..., None]
    m_t = jnp.maximum(inter, jnp.max(dlog, axis=-1))
    sc = jnp.einsum('bhtd,bhsd->bhts', q, k) * jnp.exp(dlog - m_t[..., None])
    w_inter = jnp.exp(inter - m_t)
    num = jnp.einsum('bhts,bhse->bhte', sc, v) + w_inter[..., None] * jnp.einsum('bhtd,bhde->bhte', q, c)
    den = jnp.sum(sc, axis=-1) + w_inter * jnp.einsum('bhtd,bhd->bht', q, n)
    h = num / jnp.maximum(jnp.abs(den), jnp.exp(-m_t))[..., None]
    g = bcum[..., -1:] - bcum + li
    m_new = jnp.maximum(bcum[..., -1] + m, jnp.max(g, axis=-1))
    a = jnp.exp(bcum[..., -1] + m - m_new)
    wg = jnp.exp(g - m_new[..., None])
    c_new = a[..., None, None] * c + jnp.einsum('bhs,bhsd,bhse->bhde', wg, k, v)
    n_new = a[..., None] * n + jnp.einsum('bhs,bhsd->bhd', wg, k)
    return (c_new, n_new, m_new), h


def mlstm_prompt(q, k, v, log_i, log_f):
    b, s = q.shape[:2]
    nc = s // CHUNK

    def chunks(x):
        return x.reshape((b, nc, CHUNK) + x.shape[2:]).swapaxes(0, 1).swapaxes(2, 3).astype(jnp.float32)

    init = (jnp.zeros((b, H_C, DK_C, DK_C), jnp.float32),
            jnp.zeros((b, H_C, DK_C), jnp.float32),
            jnp.zeros((b, H_C), jnp.float32))
    (c, n, m), hs = lax.scan(mlstm_chunk, init, (chunks(q), chunks(k), chunks(v), chunks(log_i), chunks(log_f)))
    h = hs.transpose(1, 0, 3, 2, 4).reshape(b, s, C_W)
    return h, c, n, m


def mlstm_sample(q, k, v, log_i, log_f, c0, n0, m0):
    def heads_first(x):
        return jnp.swapaxes(x, 1, 2).astype(jnp.float32)

    carry = (c0.astype(jnp.float32), n0.astype(jnp.float32), m0.astype(jnp.float32))
    (c, n, m), h = mlstm_chunk(carry, (heads_first(q), heads_first(k), heads_first(v),
                                       heads_first(log_i), heads_first(log_f)))
    h = h.transpose(0, 2, 1, 3).reshape(q.shape[0], q.shape[1], C_W)
    return h, c, n, m


def pool_mix(p, prev, pos0, w_pool, scale):
    b, t_len = p.shape[:2]
    ext = jnp.concatenate([prev.astype(p.dtype), p], axis=1)
    ext32 = ext.astype(jnp.float32)
    csum = jnp.concatenate([jnp.zeros((b, 1, P_W), jnp.float32), jnp.cumsum(ext32, axis=1)], axis=1)
    upto = csum[:, POOL_PREV + 1:]
    cur = ext32[:, POOL_PREV:]
    pos = (pos0 + jnp.arange(t_len)).astype(jnp.float32)
    outs = []
    for gi, win in enumerate(POOL_WINDOWS):
        sl = slice(gi * POOL_G, (gi + 1) * POOL_G)
        lo = POOL_PREV + 1 - win
        window_sum = upto[..., sl] - csum[:, lo:lo + t_len, sl]
        mean = window_sum / jnp.minimum(float(win), pos + 1.0)[None, :, None]
        outs.append(jnp.einsum('btc,cd->btd', mean - cur[..., sl], w_pool[gi].astype(jnp.float32)))
    y = jnp.concatenate(outs, axis=-1) * scale.astype(jnp.float32)
    return y.astype(p.dtype), ext[:, ext.shape[1] - POOL_PREV:]


def conv_ffn(h, w_up, conv_w, conv_b, w_down, prev):
    z = h @ w_up
    g, u = jnp.split(z, [D_FF], axis=-1)
    g_c, new_prev = causal_dwconv(g, prev, conv_w)
    return (jax.nn.silu(g_c + conv_b) * u) @ w_down, new_prev


def setup_inputs(seed: int = 0) -> dict:
    key = jax.random.key(seed)
    ks = jax.random.split(key, 32)
    f32 = jnp.float32
    n_pages = PAST_LEN // PAGE_SIZE
    n_used = DEC_BATCH * n_pages
    n_phys = n_used + n_used // 4

    def nrm(k, shape, s=1.0):
        return s * jax.random.normal(k, shape, f32)

    inputs = {}
    inputs['x_prompt'] = nrm(ks[0], (BATCH, SEQ, D_MODEL))
    inputs['x_sample'] = nrm(ks[1], (DEC_BATCH, DEC_SEQ, D_MODEL))
    inputs['cache_k'] = nrm(ks[2], (N_EVEN, n_phys, PAGE_SIZE, H_B, HD_B))
    inputs['cache_v'] = nrm(ks[3], (N_EVEN, n_phys, PAGE_SIZE, H_B, HD_B))
    inputs['cache_logf'] = jax.nn.log_sigmoid(nrm(ks[4], (N_EVEN, n_phys, PAGE_SIZE, H_B)) + 3.0)
    inputs['state_conv_a'] = nrm(ks[5], (N_EVEN, DEC_BATCH, CONV_A_W - 1, A_W))
    inputs['state_mlstm_c'] = nrm(ks[6], (N_ODD, DEC_BATCH, H_C, DK_C, DK_C), 0.1)
    inputs['state_mlstm_n'] = nrm(ks[7], (N_ODD, DEC_BATCH, H_C, DK_C))
    inputs['state_mlstm_m'] = nrm(ks[8], (N_ODD, DEC_BATCH, H_C))
    inputs['state_pool'] = nrm(ks[9], (N_ODD, DEC_BATCH, POOL_PREV, P_W))
    inputs['state_ffn_conv'] = nrm(ks[10], (DEPTH, DEC_BATCH, FFN_CONV_W - 1, D_FF))
    inputs['page_table'] = jax.random.permutation(ks[11], n_phys)[:n_used].reshape(DEC_BATCH, n_pages).astype(jnp.int32)
    inputs['norm_mix'] = 1.0 + nrm(ks[12], (DEPTH, D_MODEL), 0.02)
    inputs['norm_ffn'] = 1.0 + nrm(ks[13], (DEPTH, D_MODEL), 0.02)
    inputs['norm_final'] = 1.0 + nrm(ks[14], (D_MODEL,), 0.02)
    inputs['w_in_even'] = nrm(ks[15], (N_EVEN, D_MODEL, E_IN), D_MODEL ** -0.5)
    inputs['b_forget_even'] = jax.random.uniform(ks[16], (N_EVEN, H_B), f32, 1.0, 4.0)
    inputs['conv_a'] = nrm(ks[17], (N_EVEN, CONV_A_W, A_W), CONV_A_W ** -0.5)
    inputs['w_out_even'] = nrm(ks[18], (N_EVEN, A_W + B_W, D_MODEL), (A_W + B_W) ** -0.5)
    inputs['w_in_odd'] = nrm(ks[19], (N_ODD, D_MODEL, O_IN), D_MODEL ** -0.5)
    inputs['b_igate_odd'] = nrm(ks[20], (N_ODD, H_C), 0.1)
    inputs['b_fgate_odd'] = jax.random.uniform(ks[21], (N_ODD, H_C), f32, 3.0, 6.0)
    inputs['w_pool_odd'] = nrm(ks[22], (N_ODD, len(POOL_WINDOWS), POOL_G, POOL_G), POOL_G ** -0.5)
    inputs['pool_scale_odd'] = 1.0 + nrm(ks[23], (N_ODD, P_W), 0.1)
    inputs['w_out_odd'] = nrm(ks[24], (N_ODD, C_W + P_W, D_MODEL), (C_W + P_W) ** -0.5)
    inputs['w_up'] = nrm(ks[25], (DEPTH, D_MODEL, 2 * D_FF), D_MODEL ** -0.5)
    inputs['ffn_conv_w'] = nrm(ks[26], (DEPTH, FFN_CONV_W, D_FF), FFN_CONV_W ** -0.5)
    inputs['ffn_conv_b'] = nrm(ks[27], (DEPTH, D_FF), 0.02)
    inputs['w_down'] = nrm(ks[28], (DEPTH, D_FF, D_MODEL), D_FF ** -0.5)
    return inputs


def reference(x_prompt, x_sample, cache_k, cache_v, cache_logf, state_conv_a, state_mlstm_c,
              state_mlstm_n, state_mlstm_m, state_pool, state_ffn_conv, page_table,
              norm_mix, norm_ffn, norm_final, w_in_even, b_forget_even, conv_a, w_out_even,
              w_in_odd, b_igate_odd, b_fgate_odd, w_pool_odd, pool_scale_odd, w_out_odd,
              w_up, ffn_conv_w, ffn_conv_b, w_down):
    xp, xs = x_prompt, x_sample
    b = xp.shape[0]
    k_p, k_s, v_p, v_s, lf_p, lf_s, ca_p, ca_s = [], [], [], [], [], [], [], []
    c_p, c_s, n_p, n_s, m_p, m_s, pl_p, pl_s = [], [], [], [], [], [], [], []
    ff_p, ff_s = [], []
    for layer in range(DEPTH):
        hp = rms_norm(xp, norm_mix[layer])
        hs = rms_norm(xs, norm_mix[layer])
        if layer % 2 == 0:
            e = layer // 2
            a, q, k, v, lf, cnew = even_project(hp, w_in_even[e], b_forget_even[e], conv_a[e],
                                                jnp.zeros((b, CONV_A_W - 1, A_W), hp.dtype))
            att = fox_prompt(q, k, v, lf)
            xp = xp + jnp.concatenate([a, att], axis=-1) @ w_out_even[e]
            k_p.append(k)
            v_p.append(v)
            lf_p.append(lf)
            ca_p.append(cnew)
            a, q, k, v, lf, cnew = even_project(hs, w_in_even[e], b_forget_even[e], conv_a[e], state_conv_a[e])
            att = fox_sample(q, k, v, lf,
                             gather_pages(cache_k[e], page_table),
                             gather_pages(cache_v[e], page_table),
                             gather_pages(cache_logf[e], page_table))
            xs = xs + jnp.concatenate([a, att], axis=-1) @ w_out_even[e]
            k_s.append(k)
            v_s.append(v)
            lf_s.append(lf)
            ca_s.append(cnew)
        else:
            o = layer // 2
            q, k, v, og, li, lf, pin = odd_project(hp, w_in_odd[o], b_igate_odd[o], b_fgate_odd[o])
            hc, c, n, m = mlstm_prompt(q, k, v, li, lf)
            pout, pnew = pool_mix(pin, jnp.zeros((b, POOL_PREV, P_W), hp.dtype), 0,
                                  w_pool_odd[o], pool_scale_odd[o])
            xp = xp + jnp.concatenate([og * hc.astype(og.dtype), pout], axis=-1) @ w_out_odd[o]
            c_p.append(c)
            n_p.append(n)
            m_p.append(m)
            pl_p.append(pnew)
            q, k, v, og, li, lf, pin = odd_project(hs, w_in_odd[o], b_igate_odd[o], b_fgate_odd[o])
            hc, c, n, m = mlstm_sample(q, k, v, li, lf, state_mlstm_c[o], state_mlstm_n[o], state_mlstm_m[o])
            pout, pnew = pool_mix(pin, state_pool[o], PAST_LEN, w_pool_odd[o], pool_scale_odd[o])
            xs = xs + jnp.concatenate([og * hc.astype(og.dtype), pout], axis=-1) @ w_out_odd[o]
            c_s.append(c)
            n_s.append(n)
            m_s.append(m)
            pl_s.append(pnew)
        hp = rms_norm(xp, norm_ffn[layer])
        hs = rms_norm(xs, norm_ffn[layer])
        f, fnew = conv_ffn(hp, w_up[layer], ffn_conv_w[layer], ffn_conv_b[layer], w_down[layer],
                           jnp.zeros((b, FFN_CONV_W - 1, D_FF), hp.dtype))
        xp = xp + f
        ff_p.append(fnew)
        f, fnew = conv_ffn(hs, w_up[layer], ffn_conv_w[layer], ffn_conv_b[layer], w_down[layer],
                           state_ffn_conv[layer])
        xs = xs + f
        ff_s.append(fnew)
    y_prompt = rms_norm(xp, norm_final)
    y_sample = rms_norm(xs, norm_final)
    return (y_prompt, y_sample,
            jnp.stack(k_p), jnp.stack(k_s), jnp.stack(v_p), jnp.stack(v_s),
            jnp.stack(lf_p), jnp.stack(lf_s), jnp.stack(ca_p), jnp.stack(ca_s),
            jnp.stack(c_p), jnp.stack(c_s), jnp.stack(n_p), jnp.stack(n_s),
            jnp.stack(m_p), jnp.stack(m_s), jnp.stack(pl_p), jnp.stack(pl_s),
            jnp.stack(ff_p), jnp.stack(ff_s))
```

```python
import functools

import numpy as np
import jax
import jax.numpy as jnp
from jax import lax
from jax.experimental import pallas as pl
from jax.experimental.pallas import tpu as pltpu

D_MODEL = 1024
A_W = 512
B_W = 512
H_B = 8
HD_B = 64
C_W = 512
H_C = 4
DK_C = 128
P_W = 512
POOL_WINDOWS = (2, 4, 8, 16)
POOL_G = 128
POOL_PREV = 15
D_FF = 2816
PAGE_SIZE = 128
CHUNK = 128
EPS = 1e-6

LANES = 128
SUBLANES = 8
FF_CHUNK = 256
N_FF = D_FF // FF_CHUNK
PAGES_PER_STEP = 8
VMEM_LIMIT = 56 * 1024 * 1024
NEG = -1e30

BF = jnp.bfloat16
F32 = jnp.float32


def _rms(x, g):
    return x * lax.rsqrt(jnp.mean(x * x, axis=-1, keepdims=True) + EPS) * g


def _dot(a, b):
    return jnp.dot(a, b, preferred_element_type=F32)


def _dot_nt(a, b):
    return lax.dot_general(a, b, (((1,), (1,)), ((), ())), preferred_element_type=F32)


def _split3(x):
    hi = x.astype(BF)
    r = x - hi.astype(F32)
    mid = r.astype(BF)
    lo = (r - mid.astype(F32)).astype(BF)
    return hi, mid, lo


def _tri_dot_left(tri, x):
    hi, mid, lo = _split3(x)
    return _dot(tri, hi) + _dot(tri, mid) + _dot(tri, lo)


def _tri_dot_right(x, tri):
    hi, mid, lo = _split3(x)
    return _dot(hi, tri) + _dot(mid, tri) + _dot(lo, tri)


def _log_sigmoid(x):
    return jnp.minimum(x, 0.0) - jnp.log1p(jnp.exp(-jnp.abs(x)))


def _tri(n, lower):
    r = lax.broadcasted_iota(jnp.int32, (n, n), 0)
    c = lax.broadcasted_iota(jnp.int32, (n, n), 1)
    keep = (c <= r) if lower else (r <= c)
    return jnp.where(keep, 1.0, 0.0).astype(BF)


def _round_bf(x):
    return x.astype(BF).astype(F32)


def _const_spec(shape):
    zeros = (0,) * len(shape)
    return pl.BlockSpec(shape, lambda *_: zeros, pipeline_mode=pl.Buffered(1))


def _params(sem):
    return pltpu.CompilerParams(dimension_semantics=sem, vmem_limit_bytes=VMEM_LIMIT)


def _even_proj_kernel(x_ref, g_ref, w_ref, wf_ref, wft_ref, bf_ref, bft_ref, cw_ref,
                      a_ref, q_ref, k_ref, v_ref, kb_ref, vb_ref, lf_ref, fc_ref, fr_ref, cn_ref,
                      ext_ref, ccol_ref, crow_ref, *, tm, tk):
    t = pl.program_id(1)

    @pl.when(t == 0)
    def _():
        ext_ref[0:SUBLANES, :] = jnp.zeros((SUBLANES, A_W), F32)
        ccol_ref[...] = jnp.zeros_like(ccol_ref)
        crow_ref[...] = jnp.zeros_like(crow_ref)

    h = _rms(x_ref[...], g_ref[...]).astype(BF)
    u = _dot(h, w_ref[:, 0:A_W])
    gb = _dot(h, w_ref[:, A_W:2 * A_W])
    gc = _dot(h, w_ref[:, 2 * A_W:3 * A_W])
    cu = gc * u
    ext_ref[SUBLANES:SUBLANES + tm, :] = cu
    cw = cw_ref[...]
    conv = (cw[0:1] * ext_ref[SUBLANES - 2:SUBLANES - 2 + tm, :]
            + cw[1:2] * ext_ref[SUBLANES - 1:SUBLANES - 1 + tm, :]
            + cw[2:3] * cu)
    a_ref[...] = (gb * conv).astype(BF)
    cn_ref[...] = ext_ref[tm + SUBLANES - 2:tm + SUBLANES, :]
    ext_ref[0:SUBLANES, :] = ext_ref[tm:tm + SUBLANES, :]

    base = 3 * A_W
    q = _dot(h, w_ref[:, base:base + B_W])
    q_ref[...] = (q * (HD_B ** -0.5)).astype(BF)
    k = _dot(h, w_ref[:, base + B_W:base + 2 * B_W])
    k_ref[...] = k
    kb_ref[...] = k.astype(BF)
    v = _dot(h, w_ref[:, base + 2 * B_W:base + 3 * B_W])
    v_ref[...] = v
    vb_ref[...] = v.astype(BF)

    lf = _log_sigmoid(_dot(h, wf_ref[...]) + bf_ref[...])
    lf_ref[...] = lf[:, 0:H_B]
    cs = _tri_dot_left(_tri(tm, True), lf) + ccol_ref[0:1, :]
    fc_ref[...] = cs[:, 0:H_B]
    ccol_ref[...] = jnp.broadcast_to(cs[tm - 1:tm, :], ccol_ref.shape)

    lft = _log_sigmoid(_dot_nt(wft_ref[...], h) + bft_ref[...])
    cst = _tri_dot_right(lft, _tri(tm, False)) + crow_ref[:, 0:1]
    for j in range(tm // tk):
        fr_ref[j] = cst[0:H_B, j * tk:(j + 1) * tk]
    crow_ref[...] = jnp.broadcast_to(cst[:, tm - 1:tm], crow_ref.shape)


def _even_proj(x, g, w_main, wf, wft, bfl, bft, cw, *, tm, tk):
    b, s, _ = x.shape
    nt = s // tm
    row = lambda w: pl.BlockSpec((None, tm, w), lambda i, t: (i, t, 0))
    out_shape = (
        jax.ShapeDtypeStruct((b, s, A_W), BF),
        jax.ShapeDtypeStruct((b, s, B_W), BF),
        jax.ShapeDtypeStruct((b, s, B_W), F32),
        jax.ShapeDtypeStruct((b, s, B_W), F32),
        jax.ShapeDtypeStruct((b, s, B_W), BF),
        jax.ShapeDtypeStruct((b, s, B_W), BF),
        jax.ShapeDtypeStruct((b, s, H_B), F32),
        jax.ShapeDtypeStruct((b, s, H_B), F32),
        jax.ShapeDtypeStruct((b, s // tk, H_B, tk), F32),
        jax.ShapeDtypeStruct((b, 2, A_W), F32),
    )
    out_specs = (
        row(A_W), row(B_W), row(B_W), row(B_W), row(B_W), row(B_W), row(H_B), row(H_B),
        pl.BlockSpec((None, tm // tk, H_B, tk), lambda i, t: (i, t, 0, 0)),
        pl.BlockSpec((None, 2, A_W), lambda i, t: (i, 0, 0)),
    )
    in_specs = [
        row(D_MODEL), _const_spec(g.shape), _const_spec(w_main.shape), _const_spec(wf.shape),
        _const_spec(wft.shape), _const_spec(bfl.shape), _const_spec(bft.shape), _const_spec(cw.shape),
    ]
    return pl.pallas_call(
        functools.partial(_even_proj_kernel, tm=tm, tk=tk),
        grid=(b, nt), in_specs=in_specs, out_specs=out_specs, out_shape=out_shape,
        scratch_shapes=[pltpu.VMEM((tm + SUBLANES, A_W), F32),
                        pltpu.VMEM((SUBLANES, LANES), F32),
                        pltpu.VMEM((2 * SUBLANES, LANES), F32)],
        compiler_params=_params(("arbitrary", "arbitrary")),
        name="even_proj",
    )(x, g, w_main, wf, wft, bfl, bft, cw)


def _fox_kernel(q_ref, k_ref, v_ref, fc_ref, fr_ref, o_ref, *, tq):
    qi = pl.program_id(1)
    lane = lax.broadcasted_iota(jnp.int32, (tq, LANES), 1)
    row_i = lax.broadcasted_iota(jnp.int32, (tq, tq), 0)
    col_i = lax.broadcasted_iota(jnp.int32, (tq, tq), 1)
    causal = col_i <= row_i
    for hp in range(H_B // 2):
        lanes = slice(hp * LANES, (hp + 1) * LANES)
        qp = q_ref[:, lanes]
        outs = []
        for hh in range(2):
            h = 2 * hp + hh
            in_head = (lane < HD_B) if hh == 0 else (lane >= HD_B)
            qm = jnp.where(in_head, qp, jnp.zeros_like(qp))
            fcol = fc_ref[:, h:h + 1]

            def step(j, carry, masked):
                m, l, acc = carry
                off = pl.multiple_of(j * tq, tq)
                s = _dot_nt(qm, k_ref[pl.ds(off, tq), lanes]) + (fcol - fr_ref[j, h:h + 1, :])
                if masked:
                    s = jnp.where(causal, s, NEG)
                m_new = jnp.maximum(m, jnp.max(s, axis=-1, keepdims=True))
                p = jnp.exp(s - m_new)
                alpha = jnp.exp(m - m_new)
                l = alpha * l + jnp.sum(p, axis=-1, keepdims=True)
                acc = alpha * acc + _dot(p.astype(BF), v_ref[pl.ds(off, tq), lanes])
                return m_new, l, acc

            init = (jnp.full((tq, 1), NEG, F32), jnp.zeros((tq, 1), F32), jnp.zeros((tq, LANES), F32))
            carry = lax.fori_loop(0, qi, functools.partial(step, masked=False), init)
            _, l, acc = step(qi, carry, True)
            outs.append(acc / l)
        o_ref[:, lanes] = jnp.where(lane < HD_B, outs[0], outs[1]).astype(BF)


def _fox_prompt(q, kb, vb, fc, fr, *, tq):
    b, s, _ = q.shape
    nq = s // tq
    return pl.pallas_call(
        functools.partial(_fox_kernel, tq=tq),
        grid=(b, nq),
        in_specs=[
            pl.BlockSpec((None, tq, B_W), lambda i, t: (i, t, 0)),
            pl.BlockSpec((None, s, B_W), lambda i, t: (i, 0, 0)),
            pl.BlockSpec((None, s, B_W), lambda i, t: (i, 0, 0)),
            pl.BlockSpec((None, tq, H_B), lambda i, t: (i, t, 0)),
            pl.BlockSpec((None, nq, H_B, tq), lambda i, t: (i, 0, 0, 0)),
        ],
        out_specs=pl.BlockSpec((None, tq, B_W), lambda i, t: (i, t, 0)),
        out_shape=jax.ShapeDtypeStruct((b, s, B_W), BF),
        compiler_params=_params(("arbitrary", "arbitrary")),
        name="fox_prompt",
    )(q, kb, vb, fc, fr)


def _mix_ffn_kernel(x_ref, ma_ref, mb_ref, wo_ref, gf_ref, wug_ref, wuu_ref, cw_ref, cb_ref, wd_ref, gfin_ref,
                    y_ref, st_ref, acc_ref, h_ref, ext_ref, carry_ref, *, tm, final_norm):
    t = pl.program_id(1)

    @pl.when(t == 0)
    def _():
        carry_ref[...] = jnp.zeros_like(carry_ref)

    half = wo_ref.shape[0] // 2
    x1 = x_ref[...] + _dot(ma_ref[...], wo_ref[0:half, :]) + _dot(mb_ref[...], wo_ref[half:2 * half, :])
    acc_ref[...] = x1
    h_ref[...] = _rms(x1, gf_ref[...]).astype(BF)

    def chunk(c, _):
        h = h_ref[...]
        g = _dot(h, wug_ref[c])
        u = _dot(h, wuu_ref[c])
        ext_ref[0:SUBLANES, :] = carry_ref[c]
        ext_ref[SUBLANES:SUBLANES + tm, :] = g
        cw = cw_ref[c]
        gconv = (cw[0:1] * ext_ref[SUBLANES - 2:SUBLANES - 2 + tm, :]
                 + cw[1:2] * ext_ref[SUBLANES - 1:SUBLANES - 1 + tm, :]
                 + cw[2:3] * g + cb_ref[c])
        act = (gconv * jax.nn.sigmoid(gconv) * u).astype(BF)
        acc_ref[...] += _dot(act, wd_ref[c])
        carry_ref[c] = ext_ref[tm:tm + SUBLANES, :]
        st_ref[c] = ext_ref[tm + SUBLANES - 2:tm + SUBLANES, :]
        return 0

    lax.fori_loop(0, N_FF, chunk, 0)
    if final_norm:
        y_ref[...] = _rms(acc_ref[...], gfin_ref[...])
    else:
        y_ref[...] = acc_ref[...]


def _mix_ffn(x, ma, mb, lane_blk_b, wo, gf, wug, wuu, cw, cb, wd, gfin, *, tm, final_norm):
    b, s, _ = x.shape
    nt = s // tm
    half = wo.shape[0] // 2
    return pl.pallas_call(
        functools.partial(_mix_ffn_kernel, tm=tm, final_norm=final_norm),
        grid=(b, nt),
        in_specs=[
            pl.BlockSpec((None, tm, D_MODEL), lambda i, t: (i, t, 0)),
            pl.BlockSpec((None, tm, half), lambda i, t: (i, t, 0)),
            pl.BlockSpec((None, tm, half), lambda i, t: (i, t, lane_blk_b)),
            _const_spec(wo.shape), _const_spec(gf.shape), _const_spec(wug.shape), _const_spec(wuu.shape),
            _const_spec(cw.shape), _const_spec(cb.shape), _const_spec(wd.shape), _const_spec(gfin.shape),
        ],
        out_specs=(pl.BlockSpec((None, tm, D_MODEL), lambda i, t: (i, t, 0)),
                   pl.BlockSpec((None, N_FF, 2, FF_CHUNK), lambda i, t: (i, 0, 0, 0))),
        out_shape=(jax.ShapeDtypeStruct((b, s, D_MODEL), F32),
                   jax.ShapeDtypeStruct((b, N_FF, 2, FF_CHUNK), F32)),
        scratch_shapes=[pltpu.VMEM((tm, D_MODEL), F32), pltpu.VMEM((tm, D_MODEL), BF),
                        pltpu.VMEM((tm + SUBLANES, FF_CHUNK), F32),
                        pltpu.VMEM((N_FF, SUBLANES, FF_CHUNK), F32)],
        compiler_params=_params(("arbitrary", "arbitrary")),
        name="mix_ffn",
    )(x, ma, mb, wo, gf, wug, wuu, cw, cb, wd, gfin)


def _mix_ffn_sample_kernel(x_ref, ma_ref, mb_ref, wo_ref, gf_ref, wug_ref, wuu_ref, cw_ref, cb_ref, wd_ref,
                           gfin_ref, st_ref, y_ref, stn_ref, *, final_norm):
    half = wo_ref.shape[0] // 2
    x1 = x_ref[...] + _dot(ma_ref[...], wo_ref[0:half, :]) + _dot(mb_ref[...], wo_ref[half:2 * half, :])
    h = _rms(x1, gf_ref[...]).astype(BF)
    y_ref[...] = x1
    for c in range(N_FF):
        cols = slice(c * FF_CHUNK, (c + 1) * FF_CHUNK)
        cols1 = slice(D_FF + c * FF_CHUNK, D_FF + (c + 1) * FF_CHUNK)
        g = _dot(h, wug_ref[c])
        u = _dot(h, wuu_ref[c])
        cw = cw_ref[c]
        prev1 = st_ref[:, cols1]
        gconv = cw[0:1] * st_ref[:, cols] + cw[1:2] * prev1 + cw[2:3] * g + cb_ref[c]
        act = (gconv * jax.nn.sigmoid(gconv) * u).astype(BF)
        y_ref[...] += _dot(act, wd_ref[c])
        stn_ref[:, cols] = prev1
        stn_ref[:, cols1] = g
    if final_norm:
        y_ref[...] = _rms(y_ref[...], gfin_ref[...])


def _mix_ffn_sample(x, ma, mb, wo, gf, wug, wuu, cw, cb, wd, gfin, st, *, final_norm):
    n = x.shape[0]
    return pl.pallas_call(
        functools.partial(_mix_ffn_sample_kernel, final_norm=final_norm),
        out_shape=(jax.ShapeDtypeStruct((n, D_MODEL), F32), jax.ShapeDtypeStruct((n, 2 * D_FF), F32)),
        compiler_params=pltpu.CompilerParams(vmem_limit_bytes=VMEM_LIMIT),
        name="mix_ffn_sample",
    )(x, ma, mb, wo, gf, wug, wuu, cw, cb, wd, gfin, st)


def _odd_kernel(x_ref, g_ref, w_ref, wg_ref, wgt_ref, bg_ref, bgt_ref, wp_ref, ps_ref,
                mix_ref, c_out, n_out, m_out, pool_out,
                c_scr, n_scr, m_scr, e_ref, s_a, s_b, hc_scr, *, tm):
    t = pl.program_id(1)
    top = 3 * SUBLANES

    @pl.when(t == 0)
    def _():
        c_scr[...] = jnp.zeros_like(c_scr)
        n_scr[...] = jnp.zeros_like(n_scr)
        m_scr[...] = jnp.zeros_like(m_scr)
        e_ref[0:top, :] = jnp.zeros((top, P_W), F32)
        s_a[0:SUBLANES, :] = jnp.zeros((SUBLANES, P_W), F32)
        s_b[0:SUBLANES, :] = jnp.zeros((SUBLANES, P_W), F32)

    h = _rms(x_ref[...], g_ref[...]).astype(BF)
    q = _dot(h, w_ref[:, 0:C_W]).astype(BF)
    kf = _dot(h, w_ref[:, C_W:2 * C_W]) * (DK_C ** -0.5)
    kb = kf.astype(BF)
    vb = _dot(h, w_ref[:, 2 * C_W:3 * C_W]).astype(BF)
    og = jax.nn.sigmoid(_dot(h, w_ref[:, 3 * C_W:4 * C_W]))
    p = _dot(h, w_ref[:, 4 * C_W:4 * C_W + P_W])

    gz = _dot(h, wg_ref[...]) + bg_ref[...]
    lf_c = _log_sigmoid(gz)
    gzt = _dot_nt(wgt_ref[...], h) + bgt_ref[...]
    lf_r = _log_sigmoid(gzt)

    tri_l = _tri(CHUNK, True)
    tri_u = _tri(CHUNK, False)
    r_i = lax.broadcasted_iota(jnp.int32, (CHUNK, CHUNK), 0)
    c_i = lax.broadcasted_iota(jnp.int32, (CHUNK, CHUNK), 1)
    causal = c_i <= r_i

    for ci in range(tm // CHUNK):
        rows = slice(ci * CHUNK, (ci + 1) * CHUNK)
        bcum_c = _tri_dot_left(tri_l, lf_c[rows, :])
        bcum_r = _tri_dot_right(lf_r[:, rows], tri_u)
        for hd in range(H_C):
            lanes = slice(hd * DK_C, (hd + 1) * DK_C)
            bc = bcum_c[:, H_C + hd:H_C + hd + 1]
            br = bcum_r[H_C + hd:H_C + hd + 1, :]
            li_c = gz[rows, hd:hd + 1]
            li_r = gzt[hd:hd + 1, rows]
            m_prev = m_scr[hd:hd + 1, 0:1]
            dlog = jnp.where(causal, bc + (li_r - br), NEG)
            inter = bc + m_prev
            m_t = jnp.maximum(inter, jnp.max(dlog, axis=-1, keepdims=True))
            qh = q[rows, lanes]
            kh = kb[rows, lanes]
            vh = vb[rows, lanes]
            sc = _dot_nt(qh, kh) * jnp.exp(dlog - m_t)
            w_inter = jnp.exp(inter - m_t)
            c_old = c_scr[hd]
            n_old = n_scr[hd:hd + 1, :]
            num = _dot(sc.astype(BF), vh) + w_inter * _dot(qh, c_old.astype(BF))
            qn = jnp.sum(qh.astype(F32) * _round_bf(n_old), axis=-1, keepdims=True)
            den = jnp.sum(sc, axis=-1, keepdims=True) + w_inter * qn
            hc_scr[rows, lanes] = num / jnp.maximum(jnp.abs(den), jnp.exp(-m_t))
            bl = bc[CHUNK - 1:CHUNK, :]
            g_c = bl - bc + li_c
            m_new = jnp.maximum(bl + m_prev, jnp.max(g_c, axis=0, keepdims=True))
            a = jnp.exp(bl + m_prev - m_new)
            wg = jnp.exp(g_c - m_new)
            wgk = wg * kf[rows, lanes]
            c_scr[hd] = a * c_old + _dot(wgk.T.astype(BF), vh)
            n_scr[hd:hd + 1, :] = a * n_old + jnp.sum(_round_bf(wg) * kh.astype(F32), axis=0, keepdims=True)
            m_scr[hd:hd + 1, :] = jnp.broadcast_to(m_new, (1, LANES))

    mix_ref[:, 0:C_W] = (og * hc_scr[...]).astype(BF)

    n_rows = tm + top
    e_ref[top:n_rows, :] = p
    lo = SUBLANES
    s_a[lo:n_rows, :] = e_ref[lo:n_rows, :] + e_ref[lo - 1:n_rows - 1, :]
    s_b[lo:n_rows, :] = s_a[lo:n_rows, :] + s_a[lo - 2:n_rows - 2, :]
    w2 = s_a[top:n_rows, 0:POOL_G]
    s_a[lo:n_rows, :] = s_b[lo:n_rows, :] + s_b[lo - 4:n_rows - 4, :]
    w4 = s_b[top:n_rows, POOL_G:2 * POOL_G]
    w8 = s_a[top:n_rows, 2 * POOL_G:3 * POOL_G]
    w16 = s_a[top:n_rows, 3 * POOL_G:4 * POOL_G] + s_a[top - 8:n_rows - 8, 3 * POOL_G:4 * POOL_G]
    pos1 = (t * tm + lax.broadcasted_iota(jnp.int32, (tm, 1), 0) + 1).astype(F32)
    for gi, (win, wsum) in enumerate(zip(POOL_WINDOWS, (w2, w4, w8, w16))):
        lanes = slice(gi * POOL_G, (gi + 1) * POOL_G)
        mean = wsum / jnp.minimum(float(win), pos1)
        y = _dot((mean - p[:, lanes]).astype(BF), wp_ref[gi]) * ps_ref[:, lanes]
        mix_ref[:, C_W + gi * POOL_G:C_W + (gi + 1) * POOL_G] = y.astype(BF)
    pool_out[...] = e_ref[n_rows - 2 * SUBLANES:n_rows, :]
    e_ref[SUBLANES:top, :] = e_ref[n_rows - 2 * SUBLANES:n_rows, :]

    c_out[...] = c_scr[...]
    n_out[...] = n_scr[...]
    m_out[...] = m_scr[...]


def _odd_prompt(x, g, w_main, wg, wgt, bg, bgt, wp, ps, *, tm):
    b, s, _ = x.shape
    nt = s // tm
    state = lambda *shape: pl.BlockSpec((None,) + shape, lambda i, t: (i,) + (0,) * len(shape))
    return pl.pallas_call(
        functools.partial(_odd_kernel, tm=tm),
        grid=(b, nt),
        in_specs=[pl.BlockSpec((None, tm, D_MODEL), lambda i, t: (i, t, 0))]
                 + [_const_spec(a.shape) for a in (g, w_main, wg, wgt, bg, bgt, wp, ps)],
        out_specs=(pl.BlockSpec((None, tm, C_W + P_W), lambda i, t: (i, t, 0)),
                   state(H_C, DK_C, DK_C), state(SUBLANES, DK_C), state(SUBLANES, LANES),
                   state(2 * SUBLANES, P_W)),
        out_shape=(jax.ShapeDtypeStruct((b, s, C_W + P_W), BF),
                   jax.ShapeDtypeStruct((b, H_C, DK_C, DK_C), F32),
                   jax.ShapeDtypeStruct((b, SUBLANES, DK_C), F32),
                   jax.ShapeDtypeStruct((b, SUBLANES, LANES), F32),
                   jax.ShapeDtypeStruct((b, 2 * SUBLANES, P_W), F32)),
        scratch_shapes=[pltpu.VMEM((H_C, DK_C, DK_C), F32), pltpu.VMEM((SUBLANES, DK_C), F32),
                        pltpu.VMEM((SUBLANES, LANES), F32),
                        pltpu.VMEM((tm + 3 * SUBLANES, P_W), F32),
                        pltpu.VMEM((tm + 3 * SUBLANES, P_W), F32),
                        pltpu.VMEM((tm + 3 * SUBLANES, P_W), F32),
                        pltpu.VMEM((tm, C_W), F32)],
        compiler_params=_params(("arbitrary", "arbitrary")),
        name="odd_prompt",
    )(x, g, w_main, wg, wgt, bg, bgt, wp, ps)


def _even_sample_kernel(x_ref, g_ref, w_ref, wf_ref, bf_ref, cw_ref, st_ref, seg_ref,
                        a_ref, qrow_ref, k_ref, v_ref, lf_ref, snew_ref, cn_ref):
    n = x_ref.shape[0]
    h = _rms(x_ref[...], g_ref[...]).astype(BF)
    u = _dot(h, w_ref[:, 0:A_W])
    gb = _dot(h, w_ref[:, A_W:2 * A_W])
    gc = _dot(h, w_ref[:, 2 * A_W:3 * A_W])
    cu = gc * u
    cw = cw_ref[...]
    prev1 = st_ref[:, A_W:2 * A_W]
    a_ref[...] = (gb * (cw[0:1] * st_ref[:, 0:A_W] + cw[1:2] * prev1 + cw[2:3] * cu)).astype(BF)
    cn_ref[:, 0:A_W] = prev1
    cn_ref[:, A_W:2 * A_W] = cu
    base = 3 * A_W
    qs = _round_bf(_dot(h, w_ref[:, base:base + B_W]) * (HD_B ** -0.5))
    k = _dot(h, w_ref[:, base + B_W:base + 2 * B_W])
    v = _dot(h, w_ref[:, base + 2 * B_W:base + 3 * B_W])
    k_ref[...] = k
    v_ref[...] = v
    lf_ref[...] = _log_sigmoid(_dot(h, wf_ref[...]) + bf_ref[...])[:, 0:H_B]
    snew_ref[...] = _tri_dot_right(qs * _round_bf(k), seg_ref[...])
    lane_head = lax.broadcasted_iota(jnp.int32, (H_B, B_W), 1) // HD_B
    row_head = lax.broadcasted_iota(jnp.int32, (H_B, B_W), 0)
    own = lane_head == row_head
    for i in range(n):
        qrow_ref[i] = jnp.where(own, jnp.broadcast_to(qs[i:i + 1, :], (H_B, B_W)), 0.0).astype(BF)


def _even_sample(x, g, w_main, wf, bfl, cw, st, seg):
    n = x.shape[0]
    return pl.pallas_call(
        _even_sample_kernel,
        out_shape=(jax.ShapeDtypeStruct((n, A_W), BF),
                   jax.ShapeDtypeStruct((n, H_B, B_W), BF),
                   jax.ShapeDtypeStruct((n, B_W), F32),
                   jax.ShapeDtypeStruct((n, B_W), F32),
                   jax.ShapeDtypeStruct((n, H_B), F32),
                   jax.ShapeDtypeStruct((n, LANES), F32),
                   jax.ShapeDtypeStruct((n, 2 * A_W), F32)),
        compiler_params=pltpu.CompilerParams(vmem_limit_bytes=VMEM_LIMIT),
        name="even_sample",
    )(x, g, w_main, wf, bfl, cw, st, seg)


def _paged_kernel(pt_ref, qrow_ref, lfn_ref, snew_ref, vnew_ref, *refs, n_steps):
    npg = PAGES_PER_STEP
    k_refs = refs[0:npg]
    v_refs = refs[npg:2 * npg]
    f_refs = refs[2 * npg:3 * npg]
    o_ref = refs[3 * npg]
    m_scr, l_scr, acc_scr, suf_scr = refs[3 * npg + 1:]
    c = pl.program_id(1)
    lane_head = lax.broadcasted_iota(jnp.int32, (H_B, B_W), 1) // HD_B
    row_head = lax.broadcasted_iota(jnp.int32, (H_B, B_W), 0)
    own = lane_head == row_head

    @pl.when(c == 0)
    def _():
        m_scr[...] = jnp.broadcast_to(snew_ref[...], m_scr.shape)
        l_scr[...] = jnp.ones_like(l_scr)
        acc_scr[...] = jnp.where(own, jnp.broadcast_to(_round_bf(vnew_ref[...]), (H_B, B_W)), 0.0)
        suf_scr[...] = jnp.broadcast_to(lfn_ref[...], suf_scr.shape)

    q = qrow_ref[...]
    strict = _tri(PAGE_SIZE, True) - jnp.where(
        lax.broadcasted_iota(jnp.int32, (PAGE_SIZE, PAGE_SIZE), 0)
        == lax.broadcasted_iota(jnp.int32, (PAGE_SIZE, PAGE_SIZE), 1), 1.0, 0.0).astype(BF)
    ones = jnp.ones((PAGE_SIZE, LANES), BF)
    suf = suf_scr[:, 0:1]
    scores = []
    for j in reversed(range(npg)):
        lf = f_refs[j][...]
        within = _tri_dot_right(lf, strict)
        s = _dot_nt(q, k_refs[j][...].astype(BF)) + (within + suf)
        scores.append(s)
        suf = suf + _tri_dot_right(lf, ones)[:, 0:1]
    suf_scr[...] = jnp.broadcast_to(suf, suf_scr.shape)
    m_old = m_scr[:, 0:1]
    m_new = m_old
    for s in scores:
        m_new = jnp.maximum(m_new, jnp.max(s, axis=-1, keepdims=True))
    alpha = jnp.exp(m_old - m_new)
    l = alpha * l_scr[:, 0:1]
    acc_scr[...] = alpha * acc_scr[...]
    for s, j in zip(scores, reversed(range(npg))):
        p = jnp.exp(s - m_new)
        l = l + jnp.sum(p, axis=-1, keepdims=True)
        acc_scr[...] += _dot(p.astype(BF), v_refs[j][...].astype(BF))
    m_scr[...] = jnp.broadcast_to(m_new, m_scr.shape)
    l_scr[...] = jnp.broadcast_to(l, l_scr.shape)

    @pl.when(c == n_steps - 1)
    def _():
        o_ref[...] = jnp.sum(jnp.where(own, acc_scr[...] / l, 0.0), axis=0, keepdims=True).astype(BF)


def _paged_attention(page_table, qrow, lfn, snew, vnew, cache_k, cache_v, cache_lft):
    n, n_pages = page_table.shape
    npg = PAGES_PER_STEP
    n_steps = n_pages // npg

    def page_map(j):
        return lambda i, c, pt: (pt[i, (n_steps - 1 - c) * npg + j], 0, 0)

    per_sample = lambda *shape: pl.BlockSpec((None,) + shape, lambda i, c, pt: (i,) + (0,) * len(shape))
    in_specs = ([per_sample(H_B, B_W), per_sample(H_B, 1), per_sample(H_B, 1), per_sample(1, B_W)]
                + [pl.BlockSpec((None, PAGE_SIZE, B_W), page_map(j)) for j in range(npg)]
                + [pl.BlockSpec((None, PAGE_SIZE, B_W), page_map(j)) for j in range(npg)]
                + [pl.BlockSpec((None, H_B, PAGE_SIZE), page_map(j)) for j in range(npg)])
    grid_spec = pltpu.PrefetchScalarGridSpec(
        num_scalar_prefetch=1, grid=(n, n_steps), in_specs=in_specs,
        out_specs=per_sample(1, B_W),
        scratch_shapes=[pltpu.VMEM((H_B, LANES), F32), pltpu.VMEM((H_B, LANES), F32),
                        pltpu.VMEM((H_B, B_W), F32), pltpu.VMEM((H_B, LANES), F32)])
    return pl.pallas_call(
        functools.partial(_paged_kernel, n_steps=n_steps),
        grid_spec=grid_spec,
        out_shape=jax.ShapeDtypeStruct((n, 1, B_W), BF),
        compiler_params=_params(("arbitrary", "arbitrary")),
        name="paged_attention",
    )(page_table, qrow, lfn, snew, vnew, *([cache_k] * npg), *([cache_v] * npg), *([cache_lft] * npg))


def _odd_sample_kernel(x_ref, g_ref, w_ref, wg_ref, bg_ref, wp_ref, ps_ref, c_ref, n_ref, m_ref, pool_ref,
                       mix_ref, c_out, n_out, m_out, pool_out, *, pool_div):
    nb = x_ref.shape[0]
    h = _rms(x_ref[...], g_ref[...]).astype(BF)
    q = _round_bf(_dot(h, w_ref[:, 0:C_W]))
    kf = _dot(h, w_ref[:, C_W:2 * C_W]) * (DK_C ** -0.5)
    k = _round_bf(kf)
    v = _round_bf(_dot(h, w_ref[:, 2 * C_W:3 * C_W]))
    og = jax.nn.sigmoid(_dot(h, w_ref[:, 3 * C_W:4 * C_W]))
    p = _dot(h, w_ref[:, 4 * C_W:4 * C_W + P_W])
    gz = _dot(h, wg_ref[...]) + bg_ref[...]
    lf = _log_sigmoid(gz)
    eye = (lax.broadcasted_iota(jnp.int32, (DK_C, DK_C), 0)
           == lax.broadcasted_iota(jnp.int32, (DK_C, DK_C), 1))

    for hd in range(H_C):
        lanes = slice(hd * DK_C, (hd + 1) * DK_C)
        li = gz[:, hd:hd + 1]
        lfh = lf[:, H_C + hd:H_C + hd + 1]
        m_prev = m_ref[:, hd:hd + 1]
        qh = q[:, lanes]
        kh = k[:, lanes]
        vh = v[:, lanes]
        n_old = n_ref[:, lanes]
        m_t = jnp.maximum(lfh + m_prev, li)
        sc = jnp.sum(qh * kh, axis=-1, keepdims=True) * jnp.exp(li - m_t)
        w_inter = jnp.exp(lfh + m_prev - m_t)
        qc = jnp.concatenate(
            [_dot(qh[i:i + 1, :].astype(BF), c_ref[i, hd].astype(BF)) for i in range(nb)], axis=0)
        num = _round_bf(sc) * vh + w_inter * qc
        den = sc + w_inter * jnp.sum(qh * _round_bf(n_old), axis=-1, keepdims=True)
        hout = num / jnp.maximum(jnp.abs(den), jnp.exp(-m_t))
        mix_ref[:, lanes] = (og[:, lanes] * hout).astype(BF)
        a = w_inter
        wg = jnp.exp(li - m_t)
        wgk = wg * kf[:, lanes]
        n_out[:, lanes] = a * n_old + _round_bf(wg) * kh
        m_out[:, hd:hd + 1] = m_t
        for i in range(nb):
            kd = jnp.where(eye, jnp.broadcast_to(wgk[i:i + 1, :], (DK_C, DK_C)), 0.0).astype(BF)
            vrep = jnp.broadcast_to(vh[i:i + 1, :], (DK_C, DK_C)).astype(BF)
            c_out[i, hd] = a[i:i + 1, :] * c_ref[i, hd] + _dot(kd, vrep)

    prev = pool_ref[...]
    row = lax.broadcasted_iota(jnp.int32, prev.shape, 1)
    for gi, win in enumerate(POOL_WINDOWS):
        lanes = slice(gi * POOL_G, (gi + 1) * POOL_G)
        tail = jnp.sum(jnp.where(row >= POOL_PREV - (win - 1), prev, 0.0)[:, :, lanes], axis=1)
        mean = (tail + p[:, lanes]) / pool_div[gi]
        y = _dot((mean - p[:, lanes]).astype(BF), wp_ref[gi]) * ps_ref[:, lanes]
        mix_ref[:, C_W + gi * POOL_G:C_W + (gi + 1) * POOL_G] = y.astype(BF)
    pool_out[:, 0:POOL_PREV - 1, :] = pool_ref[:, 1:POOL_PREV, :]
    for i in range(nb):
        pool_out[i, POOL_PREV - 1:POOL_PREV, :] = p[i:i + 1, :]


def _odd_sample(x, g, w_main, wg, bg, wp, ps, c, n, m, pool, *, pool_div, nb):
    nsmp = x.shape[0]
    rows = lambda w: pl.BlockSpec((nb, w), lambda i: (i, 0))
    return pl.pallas_call(
        functools.partial(_odd_sample_kernel, pool_div=pool_div),
        grid=(nsmp // nb,),
        in_specs=[rows(D_MODEL)] + [_const_spec(a.shape) for a in (g, w_main, wg, bg, wp, ps)]
                 + [pl.BlockSpec((nb, H_C, DK_C, DK_C), lambda i: (i, 0, 0, 0)),
                    rows(C_W), rows(H_C),
                    pl.BlockSpec((nb, POOL_PREV, P_W), lambda i: (i, 0, 0))],
        out_specs=(rows(C_W + P_W),
                   pl.BlockSpec((nb, H_C, DK_C, DK_C), lambda i: (i, 0, 0, 0)),
                   rows(C_W), rows(H_C),
                   pl.BlockSpec((nb, POOL_PREV, P_W), lambda i: (i, 0, 0))),
        out_shape=(jax.ShapeDtypeStruct((nsmp, C_W + P_W), BF),
                   jax.ShapeDtypeStruct((nsmp, H_C, DK_C, DK_C), F32),
                   jax.ShapeDtypeStruct((nsmp, C_W), F32),
                   jax.ShapeDtypeStruct((nsmp, H_C), F32),
                   jax.ShapeDtypeStruct((nsmp, POOL_PREV, P_W), F32)),
        compiler_params=_params(("arbitrary",)),
        name="odd_sample",
    )(x, g, w_main, wg, bg, wp, ps, c, n, m, pool)


def _pad_cols(w, n):
    return jnp.pad(w, ((0, 0), (0, n - w.shape[1])))


def _ffn_weights(w_up, conv_w, conv_b, w_down):
    d = w_up.shape[0]
    wug = w_up[:, :D_FF].reshape(d, N_FF, FF_CHUNK).transpose(1, 0, 2).astype(BF)
    wuu = w_up[:, D_FF:].reshape(d, N_FF, FF_CHUNK).transpose(1, 0, 2).astype(BF)
    cw = conv_w.reshape(3, N_FF, FF_CHUNK).transpose(1, 0, 2)
    cb = conv_b.reshape(N_FF, 1, FF_CHUNK)
    wd = w_down.reshape(N_FF, FF_CHUNK, w_down.shape[1]).astype(BF)
    return wug, wuu, cw, cb, wd


def _ffn_state(st):
    return st.transpose(0, 2, 1, 3).reshape(st.shape[0], 2, D_FF)


def kernel(x_prompt, x_sample, cache_k, cache_v, cache_logf, state_conv_a, state_mlstm_c, state_mlstm_n,
           state_mlstm_m, state_pool, state_ffn_conv, page_table, norm_mix, norm_ffn, norm_final, w_in_even,
           b_forget_even, conv_a, w_out_even, w_in_odd, b_igate_odd, b_fgate_odd, w_pool_odd, pool_scale_odd,
           w_out_odd, w_up, ffn_conv_w, ffn_conv_b, w_down):
    b, s, _ = x_prompt.shape
    nsmp = x_sample.shape[0]
    n_pages = page_table.shape[1]
    assert x_sample.shape[1] == 1 and norm_mix.shape[0] == 2
    tm = min(512, s)
    tq = min(256, s)
    assert s % tm == 0 and n_pages % PAGES_PER_STEP == 0

    xs = x_sample.reshape(nsmp, D_MODEL)
    row = lambda v: v.reshape(1, -1)
    g_final = row(norm_final)

    w = w_in_even[0]
    n_main = 3 * A_W + 3 * B_W
    w_main = w[:, :n_main].astype(BF)
    wf = _pad_cols(w[:, n_main:], LANES).astype(BF)
    wft = jnp.pad(w[:, n_main:].T, ((0, 2 * SUBLANES - H_B), (0, 0))).astype(BF)
    bfl = _pad_cols(row(b_forget_even[0]), LANES)
    bft = jnp.pad(b_forget_even[0].reshape(H_B, 1), ((0, 2 * SUBLANES - H_B), (0, 0)))
    g_mix = row(norm_mix[0])
    wo = w_out_even[0].astype(BF)
    ffn0 = _ffn_weights(w_up[0], ffn_conv_w[0], ffn_conv_b[0], w_down[0])

    a_p, q_p, k_p, v_p, kb_p, vb_p, lf_p, fc_p, fr_p, ca_p = _even_proj(
        x_prompt, g_mix, w_main, wf, wft, bfl, bft, conv_a[0], tm=tm, tk=tq)
    att_p = _fox_prompt(q_p, kb_p, vb_p, fc_p, fr_p, tq=tq)
    xp, ff0_p = _mix_ffn(x_prompt, a_p, att_p, 0, wo, row(norm_ffn[0]), *ffn0, g_final, tm=tm, final_norm=False)

    seg = jnp.asarray((np.arange(B_W)[:, None] // HD_B == np.arange(LANES)[None, :]).astype(np.float32), dtype=BF)
    a_s, qrow_s, k_s, v_s, lf_s, snew_s, ca_s = _even_sample(
        xs, g_mix, w_main, wf, bfl, conv_a[0], state_conv_a[0].reshape(nsmp, 2 * A_W), seg)
    n_phys = cache_k.shape[1]
    att_s = _paged_attention(
        page_table, qrow_s, lf_s.reshape(nsmp, H_B, 1), snew_s[:, :H_B].reshape(nsmp, H_B, 1),
        v_s.reshape(nsmp, 1, B_W),
        cache_k[0].reshape(n_phys, PAGE_SIZE, B_W), cache_v[0].reshape(n_phys, PAGE_SIZE, B_W),
        cache_logf[0].transpose(0, 2, 1))
    xs1, ff0_s = _mix_ffn_sample(xs, a_s, att_s.reshape(nsmp, B_W), wo, row(norm_ffn[0]), *ffn0, g_final,
                                 state_ffn_conv[0].reshape(nsmp, 2 * D_FF), final_norm=False)

    w = w_in_odd[0]
    n_qkvo = 4 * C_W
    w_main = jnp.concatenate([w[:, :n_qkvo], w[:, n_qkvo + 2 * H_C:]], axis=1).astype(BF)
    w_gate = w[:, n_qkvo:n_qkvo + 2 * H_C]
    wg = _pad_cols(w_gate, LANES).astype(BF)
    wgt = jnp.pad(w_gate.T, ((0, 2 * SUBLANES - 2 * H_C), (0, 0))).astype(BF)
    b_gate = jnp.concatenate([b_igate_odd[0], b_fgate_odd[0]])
    bg = _pad_cols(row(b_gate), LANES)
    bgt = jnp.pad(b_gate.reshape(2 * H_C, 1), ((0, 2 * SUBLANES - 2 * H_C), (0, 0)))
    g_mix = row(norm_mix[1])
    wp = w_pool_odd[0].astype(BF)
    ps = row(pool_scale_odd[0])
    wo = w_out_odd[0].astype(BF)
    ffn1 = _ffn_weights(w_up[1], ffn_conv_w[1], ffn_conv_b[1], w_down[1])

    mix_p, c_p, n_p, m_p, pool_p = _odd_prompt(xp, g_mix, w_main, wg, wgt, bg, bgt, wp, ps, tm=tm)
    y_p, ff1_p = _mix_ffn(xp, mix_p, mix_p, 1, wo, row(norm_ffn[1]), *ffn1, g_final, tm=tm, final_norm=True)

    pos0 = n_pages * PAGE_SIZE
    pool_div = tuple(float(min(win, pos0 + 1)) for win in POOL_WINDOWS)
    mix_s, c_s, n_s, m_s, pool_s = _odd_sample(
        xs1, g_mix, w_main, wg, bg, wp, ps, state_mlstm_c[0], state_mlstm_n[0].reshape(nsmp, C_W),
        state_mlstm_m[0], state_pool[0], pool_div=pool_div, nb=min(8, nsmp))
    y_s, ff1_s = _mix_ffn_sample(xs1, mix_s[:, :C_W], mix_s[:, C_W:], wo, row(norm_ffn[1]), *ffn1, g_final,
                                 state_ffn_conv[1].reshape(nsmp, 2 * D_FF), final_norm=True)

    heads = lambda z: z.reshape(z.shape[:-1] + (H_B, HD_B))
    return (
        y_p, y_s.reshape(nsmp, 1, D_MODEL),
        heads(k_p)[None], heads(k_s).reshape(1, nsmp, 1, H_B, HD_B),
        heads(v_p)[None], heads(v_s).reshape(1, nsmp, 1, H_B, HD_B),
        lf_p[None], lf_s.reshape(1, nsmp, 1, H_B),
        ca_p[None], ca_s.reshape(1, nsmp, 2, A_W),
        c_p[None], c_s[None],
        n_p[:, :H_C][None], n_s.reshape(1, nsmp, H_C, DK_C),
        m_p[:, :H_C, 0][None], m_s[None],
        pool_p[:, 1:][None], pool_s[None],
        jnp.stack([_ffn_state(ff0_p), _ffn_state(ff1_p)]),
        jnp.stack([ff0_s.reshape(nsmp, 2, D_FF), ff1_s.reshape(nsmp, 2, D_FF)]),
    )
```

```python
import functools

import numpy as np
import jax
import jax.numpy as jnp
from jax import lax
from jax.experimental import pallas as pl
from jax.experimental.pallas import tpu as pltpu

D_MODEL = 1024
A_W = 512
B_W = 512
H_B = 8
HD_B = 64
C_W = 512
H_C = 4
DK_C = 128
P_W = 512
POOL_WINDOWS = (2, 4, 8, 16)
POOL_G = 128
POOL_PREV = 15
D_FF = 2816
PAGE_SIZE = 128
CHUNK = 128
EPS = 1e-6

LANES = 128
SUBLANES = 8
FF_CHUNK = 256
N_FF = D_FF // FF_CHUNK
PAGES_PER_STEP = 8
VMEM_LIMIT = 56 * 1024 * 1024
NEG = -1e30

BF = jnp.bfloat16
F32 = jnp.float32


def _rms(x, g):
    return x * lax.rsqrt(jnp.mean(x * x, axis=-1, keepdims=True) + EPS) * g


def _dot(a, b):
    return jnp.dot(a, b, preferred_element_type=F32)


def _dot_nt(a, b):
    return lax.dot_general(a, b, (((1,), (1,)), ((), ())), preferred_element_type=F32)


def _split3(x):
    hi = x.astype(BF)
    r = x - hi.astype(F32)
    mid = r.astype(BF)
    lo = (r - mid.astype(F32)).astype(BF)
    return hi, mid, lo


def _tri_dot_left(tri, x):
    hi, mid, lo = _split3(x)
    return _dot(tri, hi) + _dot(tri, mid) + _dot(tri, lo)


def _tri_dot_right(x, tri):
    hi, mid, lo = _split3(x)
    return _dot(hi, tri) + _dot(mid, tri) + _dot(lo, tri)


def _log_sigmoid(x):
    return jnp.minimum(x, 0.0) - jnp.log1p(jnp.exp(-jnp.abs(x)))


def _tri(n, lower):
    r = lax.broadcasted_iota(jnp.int32, (n, n), 0)
    c = lax.broadcasted_iota(jnp.int32, (n, n), 1)
    keep = (c <= r) if lower else (r <= c)
    return jnp.where(keep, 1.0, 0.0).astype(BF)


def _round_bf(x):
    return x.astype(BF).astype(F32)


def _const_spec(shape):
    zeros = (0,) * len(shape)
    return pl.BlockSpec(shape, lambda *_: zeros, pipeline_mode=pl.Buffered(1))


def _params(sem):
    return pltpu.CompilerParams(dimension_semantics=sem, vmem_limit_bytes=VMEM_LIMIT)


def _even_proj_kernel(x_ref, g_ref, w_ref, wqt_ref, wvt_ref, wf_ref, wft_ref, bf_ref, bft_ref, cw_ref,
                      a_ref, qt_ref, k_ref, v_ref, kb_ref, vt_ref, lf_ref, fc_ref, fr_ref, cn_ref,
                      ext_ref, ccol_ref, crow_ref, *, tm, tk):
    t = pl.program_id(1)

    @pl.when(t == 0)
    def _():
        ext_ref[0:SUBLANES, :] = jnp.zeros((SUBLANES, A_W), F32)
        ccol_ref[...] = jnp.zeros_like(ccol_ref)
        crow_ref[...] = jnp.zeros_like(crow_ref)

    h = _rms(x_ref[...], g_ref[...]).astype(BF)
    u = _dot(h, w_ref[:, 0:A_W])
    gb = _dot(h, w_ref[:, A_W:2 * A_W])
    gc = _dot(h, w_ref[:, 2 * A_W:3 * A_W])
    cu = gc * u
    ext_ref[SUBLANES:SUBLANES + tm, :] = cu
    cw = cw_ref[...]
    conv = (cw[0:1] * ext_ref[SUBLANES - 2:SUBLANES - 2 + tm, :]
            + cw[1:2] * ext_ref[SUBLANES - 1:SUBLANES - 1 + tm, :]
            + cw[2:3] * cu)
    a_ref[...] = (gb * conv).astype(BF)
    cn_ref[...] = ext_ref[tm + SUBLANES - 2:tm + SUBLANES, :]
    ext_ref[0:SUBLANES, :] = ext_ref[tm:tm + SUBLANES, :]

    base = 3 * A_W
    k = _dot(h, w_ref[:, base + B_W:base + 2 * B_W])
    k_ref[...] = k
    kb_ref[...] = k.astype(BF)
    v_ref[...] = _dot(h, w_ref[:, base + 2 * B_W:base + 3 * B_W])
    qt = (_dot_nt(wqt_ref[...], h) * (HD_B ** -0.5)).astype(BF)
    vt = _dot_nt(wvt_ref[...], h).astype(BF)
    for j in range(tm // tk):
        qt_ref[j] = qt[:, j * tk:(j + 1) * tk]
        vt_ref[j] = vt[:, j * tk:(j + 1) * tk]

    lf = _log_sigmoid(_dot(h, wf_ref[...]) + bf_ref[...])
    lf_ref[...] = lf[:, 0:H_B]
    cs = _tri_dot_left(_tri(tm, True), lf) + ccol_ref[0:1, :]
    fc_ref[...] = cs[:, 0:H_B]
    ccol_ref[...] = jnp.broadcast_to(cs[tm - 1:tm, :], ccol_ref.shape)

    lft = _log_sigmoid(_dot_nt(wft_ref[...], h) + bft_ref[...])
    cst = _tri_dot_right(lft, _tri(tm, False)) + crow_ref[:, 0:1]
    for j in range(tm // tk):
        fr_ref[j] = cst[0:H_B, j * tk:(j + 1) * tk]
    crow_ref[...] = jnp.broadcast_to(cst[:, tm - 1:tm], crow_ref.shape)


def _even_proj(x, g, w_main, wqt, wvt, wf, wft, bfl, bft, cw, *, tm, tk):
    b, s, _ = x.shape
    nt = s // tm
    row = lambda w: pl.BlockSpec((None, tm, w), lambda i, t: (i, t, 0))
    chunked = lambda rows: pl.BlockSpec((None, tm // tk, rows, tk), lambda i, t: (i, t, 0, 0))
    out_shape = (
        jax.ShapeDtypeStruct((b, s, A_W), BF),
        jax.ShapeDtypeStruct((b, s // tk, B_W, tk), BF),
        jax.ShapeDtypeStruct((b, s, B_W), F32),
        jax.ShapeDtypeStruct((b, s, B_W), F32),
        jax.ShapeDtypeStruct((b, s, B_W), BF),
        jax.ShapeDtypeStruct((b, s // tk, B_W, tk), BF),
        jax.ShapeDtypeStruct((b, s, H_B), F32),
        jax.ShapeDtypeStruct((b, s, H_B), F32),
        jax.ShapeDtypeStruct((b, s // tk, H_B, tk), F32),
        jax.ShapeDtypeStruct((b, 2, A_W), F32),
    )
    out_specs = (
        row(A_W), chunked(B_W), row(B_W), row(B_W), row(B_W), chunked(B_W), row(H_B), row(H_B),
        chunked(H_B),
        pl.BlockSpec((None, 2, A_W), lambda i, t: (i, 0, 0)),
    )
    in_specs = [row(D_MODEL)] + [_const_spec(a.shape) for a in (g, w_main, wqt, wvt, wf, wft, bfl, bft, cw)]
    return pl.pallas_call(
        functools.partial(_even_proj_kernel, tm=tm, tk=tk),
        grid=(b, nt), in_specs=in_specs, out_specs=out_specs, out_shape=out_shape,
        scratch_shapes=[pltpu.VMEM((tm + SUBLANES, A_W), F32),
                        pltpu.VMEM((SUBLANES, LANES), F32),
                        pltpu.VMEM((2 * SUBLANES, LANES), F32)],
        compiler_params=_params(("arbitrary", "arbitrary")),
        name="even_proj",
    )(x, g, w_main, wqt, wvt, wf, wft, bfl, bft, cw)


def _fox_kernel(qt_ref, k_ref, vt_ref, fc_ref, fr_ref, o_ref, qm_scr, m_scr, l_scr, acc_scr, *, tq):
    qi = pl.program_id(1)
    sub = lax.broadcasted_iota(jnp.int32, (LANES, tq), 0)
    for h in range(H_B):
        hp, hh = divmod(h, 2)
        qp = qt_ref[hp * LANES:(hp + 1) * LANES, :]
        in_head = (sub < HD_B) if hh == 0 else (sub >= HD_B)
        qm_scr[h] = jnp.where(in_head, qp, jnp.zeros_like(qp))
    m_scr[...] = jnp.full(m_scr.shape, NEG, F32)
    l_scr[...] = jnp.zeros_like(l_scr)
    acc_scr[...] = jnp.zeros_like(acc_scr)
    causal = (lax.broadcasted_iota(jnp.int32, (tq, tq), 0) <= lax.broadcasted_iota(jnp.int32, (tq, tq), 1))

    def step(j, masked):
        off = pl.multiple_of(j * tq, tq)
        group = lambda h: slice((h // 2) * LANES, (h // 2 + 1) * LANES)
        scores = [_dot(k_ref[pl.ds(off, tq), group(h)], qm_scr[h]) for h in range(H_B)]
        probs, alphas = [], []
        for h in range(H_B):
            s = scores[h] + (fr_ref[h:h + 1, :] - fc_ref[pl.ds(off, tq), h:h + 1])
            if masked:
                s = jnp.where(causal, s, NEG)
            m_old = m_scr[h:h + 1, :]
            m_new = jnp.maximum(m_old, jnp.max(s, axis=0, keepdims=True))
            p = jnp.exp(s - m_new)
            alpha = jnp.exp(m_old - m_new)
            l_scr[h:h + 1, :] = alpha * l_scr[h:h + 1, :] + jnp.sum(p, axis=0, keepdims=True)
            m_scr[h:h + 1, :] = m_new
            probs.append(p.astype(BF))
            alphas.append(alpha)
        for h in range(H_B):
            hp, hh = divmod(h, 2)
            rows = slice(hh * HD_B, (hh + 1) * HD_B)
            pv = _dot(vt_ref[j, group(h), :], probs[h])
            acc_scr[hp, rows, :] = alphas[h] * acc_scr[hp, rows, :] + pv[rows, :]

    def body(j, carry):
        step(j, False)
        return carry

    lax.fori_loop(0, qi, body, 0)
    step(qi, True)
    for hp in range(H_B // 2):
        denom = jnp.where(sub < HD_B, l_scr[2 * hp:2 * hp + 1, :], l_scr[2 * hp + 1:2 * hp + 2, :])
        o_ref[:, hp * LANES:(hp + 1) * LANES] = (acc_scr[hp] / denom).T.astype(BF)


def _fox_prompt(qt, kb, vt, fc, fr, *, tq):
    b, nq, _, _ = qt.shape
    s = nq * tq
    return pl.pallas_call(
        functools.partial(_fox_kernel, tq=tq),
        grid=(b, nq),
        in_specs=[
            pl.BlockSpec((None, None, B_W, tq), lambda i, t: (i, t, 0, 0)),
            pl.BlockSpec((None, s, B_W), lambda i, t: (i, 0, 0)),
            pl.BlockSpec((None, nq, B_W, tq), lambda i, t: (i, 0, 0, 0)),
            pl.BlockSpec((None, s, H_B), lambda i, t: (i, 0, 0)),
            pl.BlockSpec((None, None, H_B, tq), lambda i, t: (i, t, 0, 0)),
        ],
        out_specs=pl.BlockSpec((None, tq, B_W), lambda i, t: (i, t, 0)),
        out_shape=jax.ShapeDtypeStruct((b, s, B_W), BF),
        scratch_shapes=[pltpu.VMEM((H_B, LANES, tq), BF), pltpu.VMEM((H_B, tq), F32),
                        pltpu.VMEM((H_B, tq), F32), pltpu.VMEM((H_B // 2, LANES, tq), F32)],
        compiler_params=_params(("arbitrary", "arbitrary")),
        name="fox_prompt",
    )(qt, kb, vt, fc, fr)


def _mix_ffn_kernel(x_ref, ma_ref, mb_ref, wo_ref, gf_ref, wug_ref, wuu_ref, cw_ref, cb_ref, wd_ref, gfin_ref,
                    y_ref, st_ref, acc_ref, h_ref, ext_ref, carry_ref, *, tm, final_norm):
    t = pl.program_id(1)

    @pl.when(t == 0)
    def _():
        carry_ref[...] = jnp.zeros_like(carry_ref)

    half = wo_ref.shape[0] // 2
    x1 = x_ref[...] + _dot(ma_ref[...], wo_ref[0:half, :]) + _dot(mb_ref[...], wo_ref[half:2 * half, :])
    acc_ref[...] = x1
    h_ref[...] = _rms(x1, gf_ref[...]).astype(BF)

    def chunk(c, _):
        h = h_ref[...]
        g = _dot(h, wug_ref[c])
        u = _dot(h, wuu_ref[c])
        ext_ref[0:SUBLANES, :] = carry_ref[c]
        ext_ref[SUBLANES:SUBLANES + tm, :] = g
        cw = cw_ref[c]
        gconv = (cw[0:1] * ext_ref[SUBLANES - 2:SUBLANES - 2 + tm, :]
                 + cw[1:2] * ext_ref[SUBLANES - 1:SUBLANES - 1 + tm, :]
                 + cw[2:3] * g + cb_ref[c])
        act = (gconv * jax.nn.sigmoid(gconv) * u).astype(BF)
        acc_ref[...] += _dot(act, wd_ref[c])
        carry_ref[c] = ext_ref[tm:tm + SUBLANES, :]
        st_ref[c] = ext_ref[tm + SUBLANES - 2:tm + SUBLANES, :]
        return 0

    lax.fori_loop(0, N_FF, chunk, 0)
    if final_norm:
        y_ref[...] = _rms(acc_ref[...], gfin_ref[...])
    else:
        y_ref[...] = acc_ref[...]


def _mix_ffn(x, ma, mb, lane_blk_b, wo, gf, wug, wuu, cw, cb, wd, gfin, *, tm, final_norm):
    b, s, _ = x.shape
    nt = s // tm
    half = wo.shape[0] // 2
    return pl.pallas_call(
        functools.partial(_mix_ffn_kernel, tm=tm, final_norm=final_norm),
        grid=(b, nt),
        in_specs=[
            pl.BlockSpec((None, tm, D_MODEL), lambda i, t: (i, t, 0)),
            pl.BlockSpec((None, tm, half), lambda i, t: (i, t, 0)),
            pl.BlockSpec((None, tm, half), lambda i, t: (i, t, lane_blk_b)),
            _const_spec(wo.shape), _const_spec(gf.shape), _const_spec(wug.shape), _const_spec(wuu.shape),
            _const_spec(cw.shape), _const_spec(cb.shape), _const_spec(wd.shape), _const_spec(gfin.shape),
        ],
        out_specs=(pl.BlockSpec((None, tm, D_MODEL), lambda i, t: (i, t, 0)),
                   pl.BlockSpec((None, N_FF, 2, FF_CHUNK), lambda i, t: (i, 0, 0, 0))),
        out_shape=(jax.ShapeDtypeStruct((b, s, D_MODEL), F32),
                   jax.ShapeDtypeStruct((b, N_FF, 2, FF_CHUNK), F32)),
        scratch_shapes=[pltpu.VMEM((tm, D_MODEL), F32), pltpu.VMEM((tm, D_MODEL), BF),
                        pltpu.VMEM((tm + SUBLANES, FF_CHUNK), F32),
                        pltpu.VMEM((N_FF, SUBLANES, FF_CHUNK), F32)],
        compiler_params=_params(("arbitrary", "arbitrary")),
        name="mix_ffn",
    )(x, ma, mb, wo, gf, wug, wuu, cw, cb, wd, gfin)


def _mix_ffn_sample_kernel(x_ref, ma_ref, mb_ref, wo_ref, gf_ref, wug_ref, wuu_ref, cw_ref, cb_ref, wd_ref,
                           gfin_ref, st_ref, y_ref, stn_ref, *, final_norm):
    half = wo_ref.shape[0] // 2
    x1 = x_ref[...] + _dot(ma_ref[...], wo_ref[0:half, :]) + _dot(mb_ref[...], wo_ref[half:2 * half, :])
    h = _rms(x1, gf_ref[...]).astype(BF)
    y_ref[...] = x1
    for c in range(N_FF):
        cols = slice(c * FF_CHUNK, (c + 1) * FF_CHUNK)
        cols1 = slice(D_FF + c * FF_CHUNK, D_FF + (c + 1) * FF_CHUNK)
        g = _dot(h, wug_ref[c])
        u = _dot(h, wuu_ref[c])
        cw = cw_ref[c]
        prev1 = st_ref[:, cols1]
        gconv = cw[0:1] * st_ref[:, cols] + cw[1:2] * prev1 + cw[2:3] * g + cb_ref[c]
        act = (gconv * jax.nn.sigmoid(gconv) * u).astype(BF)
        y_ref[...] += _dot(act, wd_ref[c])
        stn_ref[:, cols] = prev1
        stn_ref[:, cols1] = g
    if final_norm:
        y_ref[...] = _rms(y_ref[...], gfin_ref[...])


def _mix_ffn_sample(x, ma, mb, wo, gf, wug, wuu, cw, cb, wd, gfin, st, *, final_norm):
    n = x.shape[0]
    return pl.pallas_call(
        functools.partial(_mix_ffn_sample_kernel, final_norm=final_norm),
        out_shape=(jax.ShapeDtypeStruct((n, D_MODEL), F32), jax.ShapeDtypeStruct((n, 2 * D_FF), F32)),
        compiler_params=pltpu.CompilerParams(vmem_limit_bytes=VMEM_LIMIT),
        name="mix_ffn_sample",
    )(x, ma, mb, wo, gf, wug, wuu, cw, cb, wd, gfin, st)


def _odd_kernel(x_ref, g_ref, w_ref, wg_ref, wgt_ref, bg_ref, bgt_ref, wp_ref, ps_ref,
                mix_ref, c_out, n_out, m_out, pool_out,
                c_scr, n_scr, m_scr, e_ref, s_a, s_b, hc_scr, *, tm):
    t = pl.program_id(1)
    top = 3 * SUBLANES

    @pl.when(t == 0)
    def _():
        c_scr[...] = jnp.zeros_like(c_scr)
        n_scr[...] = jnp.zeros_like(n_scr)
        m_scr[...] = jnp.zeros_like(m_scr)
        e_ref[0:top, :] = jnp.zeros((top, P_W), F32)
        s_a[0:SUBLANES, :] = jnp.zeros((SUBLANES, P_W), F32)
        s_b[0:SUBLANES, :] = jnp.zeros((SUBLANES, P_W), F32)

    h = _rms(x_ref[...], g_ref[...]).astype(BF)
    q = _dot(h, w_ref[:, 0:C_W]).astype(BF)
    kf = _dot(h, w_ref[:, C_W:2 * C_W]) * (DK_C ** -0.5)
    kb = kf.astype(BF)
    vb = _dot(h, w_ref[:, 2 * C_W:3 * C_W]).astype(BF)
    og = jax.nn.sigmoid(_dot(h, w_ref[:, 3 * C_W:4 * C_W]))
    p = _dot(h, w_ref[:, 4 * C_W:4 * C_W + P_W])

    gz = _dot(h, wg_ref[...]) + bg_ref[...]
    lf_c = _log_sigmoid(gz)
    gzt = _dot_nt(wgt_ref[...], h) + bgt_ref[...]
    lf_r = _log_sigmoid(gzt)

    tri_l = _tri(CHUNK, True)
    tri_u = _tri(CHUNK, False)
    r_i = lax.broadcasted_iota(jnp.int32, (CHUNK, CHUNK), 0)
    c_i = lax.broadcasted_iota(jnp.int32, (CHUNK, CHUNK), 1)
    causal = c_i <= r_i

    for ci in range(tm // CHUNK):
        rows = slice(ci * CHUNK, (ci + 1) * CHUNK)
        bcum_c = _tri_dot_left(tri_l, lf_c[rows, :])
        bcum_r = _tri_dot_right(lf_r[:, rows], tri_u)
        for hd in range(H_C):
            lanes = slice(hd * DK_C, (hd + 1) * DK_C)
            bc = bcum_c[:, H_C + hd:H_C + hd + 1]
            br = bcum_r[H_C + hd:H_C + hd + 1, :]
            li_c = gz[rows, hd:hd + 1]
            li_r = gzt[hd:hd + 1, rows]
            m_prev = m_scr[hd:hd + 1, 0:1]
            dlog = jnp.where(causal, bc + (li_r - br), NEG)
            inter = bc + m_prev
            m_t = jnp.maximum(inter, jnp.max(dlog, axis=-1, keepdims=True))
            qh = q[rows, lanes]
            kh = kb[rows, lanes]
            vh = vb[rows, lanes]
            sc = _dot_nt(qh, kh) * jnp.exp(dlog - m_t)
            w_inter = jnp.exp(inter - m_t)
            c_old = c_scr[hd]
            n_old = n_scr[hd:hd + 1, :]
            num = _dot(sc.astype(BF), vh) + w_inter * _dot(qh, c_old.astype(BF))
            qn = jnp.sum(qh.astype(F32) * _round_bf(n_old), axis=-1, keepdims=True)
            den = jnp.sum(sc, axis=-1, keepdims=True) + w_inter * qn
            hc_scr[rows, lanes] = num / jnp.maximum(jnp.abs(den), jnp.exp(-m_t))
            bl = bc[CHUNK - 1:CHUNK, :]
            g_c = bl - bc + li_c
            m_new = jnp.maximum(bl + m_prev, jnp.max(g_c, axis=0, keepdims=True))
            a = jnp.exp(bl + m_prev - m_new)
            wg = jnp.exp(g_c - m_new)
            wgk = wg * kf[rows, lanes]
            c_scr[hd] = a * c_old + _dot(wgk.T.astype(BF), vh)
            n_scr[hd:hd + 1, :] = a * n_old + jnp.sum(_round_bf(wg) * kh.astype(F32), axis=0, keepdims=True)
            m_scr[hd:hd + 1, :] = jnp.broadcast_to(m_new, (1, LANES))

    mix_ref[:, 0:C_W] = (og * hc_scr[...]).astype(BF)

    n_rows = tm + top
    e_ref[top:n_rows, :] = p
    lo = SUBLANES
    s_a[lo:n_rows, :] = e_ref[lo:n_rows, :] + e_ref[lo - 1:n_rows - 1, :]
    s_b[lo:n_rows, :] = s_a[lo:n_rows, :] + s_a[lo - 2:n_rows - 2, :]
    w2 = s_a[top:n_rows, 0:POOL_G]
    s_a[lo:n_rows, :] = s_b[lo:n_rows, :] + s_b[lo - 4:n_rows - 4, :]
    w4 = s_b[top:n_rows, POOL_G:2 * POOL_G]
    w8 = s_a[top:n_rows, 2 * POOL_G:3 * POOL_G]
    w16 = s_a[top:n_rows, 3 * POOL_G:4 * POOL_G] + s_a[top - 8:n_rows - 8, 3 * POOL_G:4 * POOL_G]
    pos1 = (t * tm + lax.broadcasted_iota(jnp.int32, (tm, 1), 0) + 1).astype(F32)
    for gi, (win, wsum) in enumerate(zip(POOL_WINDOWS, (w2, w4, w8, w16))):
        lanes = slice(gi * POOL_G, (gi + 1) * POOL_G)
        mean = wsum / jnp.minimum(float(win), pos1)
        y = _dot((mean - p[:, lanes]).astype(BF), wp_ref[gi]) * ps_ref[:, lanes]
        mix_ref[:, C_W + gi * POOL_G:C_W + (gi + 1) * POOL_G] = y.astype(BF)
    pool_out[...] = e_ref[n_rows - 2 * SUBLANES:n_rows, :]
    e_ref[SUBLANES:top, :] = e_ref[n_rows - 2 * SUBLANES:n_rows, :]

    c_out[...] = c_scr[...]
    n_out[...] = n_scr[...]
    m_out[...] = m_scr[...]


def _odd_prompt(x, g, w_main, wg, wgt, bg, bgt, wp, ps, *, tm):
    b, s, _ = x.shape
    nt = s // tm
    state = lambda *shape: pl.BlockSpec((None,) + shape, lambda i, t: (i,) + (0,) * len(shape))
    return pl.pallas_call(
        functools.partial(_odd_kernel, tm=tm),
        grid=(b, nt),
        in_specs=[pl.BlockSpec((None, tm, D_MODEL), lambda i, t: (i, t, 0))]
                 + [_const_spec(a.shape) for a in (g, w_main, wg, wgt, bg, bgt, wp, ps)],
        out_specs=(pl.BlockSpec((None, tm, C_W + P_W), lambda i, t: (i, t, 0)),
                   state(H_C, DK_C, DK_C), state(SUBLANES, DK_C), state(SUBLANES, LANES),
                   state(2 * SUBLANES, P_W)),
        out_shape=(jax.ShapeDtypeStruct((b, s, C_W + P_W), BF),
                   jax.ShapeDtypeStruct((b, H_C, DK_C, DK_C), F32),
                   jax.ShapeDtypeStruct((b, SUBLANES, DK_C), F32),
                   jax.ShapeDtypeStruct((b, SUBLANES, LANES), F32),
                   jax.ShapeDtypeStruct((b, 2 * SUBLANES, P_W), F32)),
        scratch_shapes=[pltpu.VMEM((H_C, DK_C, DK_C), F32), pltpu.VMEM((SUBLANES, DK_C), F32),
                        pltpu.VMEM((SUBLANES, LANES), F32),
                        pltpu.VMEM((tm + 3 * SUBLANES, P_W), F32),
                        pltpu.VMEM((tm + 3 * SUBLANES, P_W), F32),
                        pltpu.VMEM((tm + 3 * SUBLANES, P_W), F32),
                        pltpu.VMEM((tm, C_W), F32)],
        compiler_params=_params(("arbitrary", "arbitrary")),
        name="odd_prompt",
    )(x, g, w_main, wg, wgt, bg, bgt, wp, ps)


def _even_sample_kernel(x_ref, g_ref, w_ref, wf_ref, bf_ref, cw_ref, st_ref, seg_ref,
                        a_ref, q_ref, k_ref, v_ref, lf_ref, snew_ref, cn_ref):
    h = _rms(x_ref[...], g_ref[...]).astype(BF)
    u = _dot(h, w_ref[:, 0:A_W])
    gb = _dot(h, w_ref[:, A_W:2 * A_W])
    gc = _dot(h, w_ref[:, 2 * A_W:3 * A_W])
    cu = gc * u
    cw = cw_ref[...]
    prev1 = st_ref[:, A_W:2 * A_W]
    a_ref[...] = (gb * (cw[0:1] * st_ref[:, 0:A_W] + cw[1:2] * prev1 + cw[2:3] * cu)).astype(BF)
    cn_ref[:, 0:A_W] = prev1
    cn_ref[:, A_W:2 * A_W] = cu
    base = 3 * A_W
    qs = _round_bf(_dot(h, w_ref[:, base:base + B_W]) * (HD_B ** -0.5))
    k = _dot(h, w_ref[:, base + B_W:base + 2 * B_W])
    v = _dot(h, w_ref[:, base + 2 * B_W:base + 3 * B_W])
    k_ref[...] = k
    v_ref[...] = v
    lf_ref[...] = _log_sigmoid(_dot(h, wf_ref[...]) + bf_ref[...])[:, 0:H_B]
    snew_ref[...] = _tri_dot_right(qs * _round_bf(k), seg_ref[...])
    q_ref[...] = qs


def _even_sample(x, g, w_main, wf, bfl, cw, st, seg):
    n = x.shape[0]
    return pl.pallas_call(
        _even_sample_kernel,
        out_shape=(jax.ShapeDtypeStruct((n, A_W), BF),
                   jax.ShapeDtypeStruct((n, B_W), F32),
                   jax.ShapeDtypeStruct((n, B_W), F32),
                   jax.ShapeDtypeStruct((n, B_W), F32),
                   jax.ShapeDtypeStruct((n, H_B), F32),
                   jax.ShapeDtypeStruct((n, LANES), F32),
                   jax.ShapeDtypeStruct((n, 2 * A_W), F32)),
        compiler_params=pltpu.CompilerParams(vmem_limit_bytes=VMEM_LIMIT),
        name="even_sample",
    )(x, g, w_main, wf, bfl, cw, st, seg)


def _paged_kernel(pt_ref, qsel_ref, lfn_ref, snew_ref, vnew_ref, *refs, n_steps):
    npg = PAGES_PER_STEP
    k_refs = refs[0:npg]
    v_refs = refs[npg:2 * npg]
    f_refs = refs[2 * npg:3 * npg]
    o_ref = refs[3 * npg]
    m_scr, l_scr, acc_scr, suf_scr = refs[3 * npg + 1:]
    c = pl.program_id(1)
    head_row = lax.broadcasted_iota(jnp.int32, (H_B, PAGE_SIZE), 0)
    head_row_v = lax.broadcasted_iota(jnp.int32, (H_B, HD_B), 0)

    def head_rows(ref, h):
        return ref[pl.ds(h, PAGE_SIZE, stride=H_B), :].astype(BF)

    @pl.when(c == 0)
    def _():
        m_scr[...] = jnp.broadcast_to(snew_ref[...], m_scr.shape)
        l_scr[...] = jnp.ones_like(l_scr)
        acc_scr[...] = _round_bf(vnew_ref[...])
        suf_scr[...] = jnp.broadcast_to(lfn_ref[...], suf_scr.shape)

    strict = _tri(PAGE_SIZE, True) - jnp.where(
        lax.broadcasted_iota(jnp.int32, (PAGE_SIZE, PAGE_SIZE), 0)
        == lax.broadcasted_iota(jnp.int32, (PAGE_SIZE, PAGE_SIZE), 1), 1.0, 0.0).astype(BF)
    ones = jnp.ones((PAGE_SIZE, LANES), BF)
    suf = suf_scr[:, 0:1]
    scores = []
    for j in reversed(range(npg)):
        lf = f_refs[j][...]
        within = _tri_dot_right(lf, strict)
        qk = _dot_nt(qsel_ref[0], head_rows(k_refs[j], 0))
        for h in range(1, H_B):
            qk = jnp.where(head_row == h, _dot_nt(qsel_ref[h], head_rows(k_refs[j], h)), qk)
        scores.append(qk + (within + suf))
        suf = suf + _tri_dot_right(lf, ones)[:, 0:1]
    suf_scr[...] = jnp.broadcast_to(suf, suf_scr.shape)
    m_old = m_scr[:, 0:1]
    m_new = m_old
    for s in scores:
        m_new = jnp.maximum(m_new, jnp.max(s, axis=-1, keepdims=True))
    alpha = jnp.exp(m_old - m_new)
    l = alpha * l_scr[:, 0:1]
    acc = alpha * acc_scr[...]
    for s, j in zip(scores, reversed(range(npg))):
        p = jnp.exp(s - m_new)
        l = l + jnp.sum(p, axis=-1, keepdims=True)
        pv = _dot(jnp.where(head_row == 0, p, 0.0).astype(BF), head_rows(v_refs[j], 0))
        for h in range(1, H_B):
            ph = jnp.where(head_row == h, p, 0.0).astype(BF)
            pv = jnp.where(head_row_v == h, _dot(ph, head_rows(v_refs[j], h)), pv)
        acc = acc + pv
    acc_scr[...] = acc
    m_scr[...] = jnp.broadcast_to(m_new, m_scr.shape)
    l_scr[...] = jnp.broadcast_to(l, l_scr.shape)

    @pl.when(c == n_steps - 1)
    def _():
        o_ref[...] = acc / l


def _paged_attention(page_table, qsel, lfn, snew, vnew, cache_k, cache_v, cache_lft):
    n, n_pages = page_table.shape
    npg = PAGES_PER_STEP
    n_steps = n_pages // npg
    page_rows = PAGE_SIZE * H_B

    def page_of(i, c, pt, j):
        return pt[i, (n_steps - 1 - c) * npg + j]

    per_sample = lambda *shape: pl.BlockSpec((None,) + shape, lambda i, c, pt: (i,) + (0,) * len(shape))
    kv_spec = lambda j: pl.BlockSpec((page_rows, HD_B), lambda i, c, pt: (page_of(i, c, pt, j), 0))
    lf_spec = lambda j: pl.BlockSpec((None, H_B, PAGE_SIZE), lambda i, c, pt: (page_of(i, c, pt, j), 0, 0))
    in_specs = ([per_sample(H_B, H_B, HD_B), per_sample(H_B, 1), per_sample(H_B, 1), per_sample(H_B, HD_B)]
                + [kv_spec(j) for j in range(npg)] + [kv_spec(j) for j in range(npg)]
                + [lf_spec(j) for j in range(npg)])
    grid_spec = pltpu.PrefetchScalarGridSpec(
        num_scalar_prefetch=1, grid=(n, n_steps), in_specs=in_specs,
        out_specs=per_sample(H_B, HD_B),
        scratch_shapes=[pltpu.VMEM((H_B, LANES), F32), pltpu.VMEM((H_B, LANES), F32),
                        pltpu.VMEM((H_B, HD_B), F32), pltpu.VMEM((H_B, LANES), F32)])
    return pl.pallas_call(
        functools.partial(_paged_kernel, n_steps=n_steps),
        grid_spec=grid_spec,
        out_shape=jax.ShapeDtypeStruct((n, H_B, HD_B), F32),
        compiler_params=_params(("arbitrary", "arbitrary")),
        name="paged_attention",
    )(page_table, qsel, lfn, snew, vnew, *([cache_k] * npg), *([cache_v] * npg), *([cache_lft] * npg))


def _odd_sample_kernel(x_ref, g_ref, w_ref, wg_ref, bg_ref, wp_ref, ps_ref, c_ref, n_ref, m_ref, pool_ref,
                       mix_ref, c_out, n_out, m_out, pool_out, *, pool_div):
    nb = x_ref.shape[0]
    h = _rms(x_ref[...], g_ref[...]).astype(BF)
    q = _round_bf(_dot(h, w_ref[:, 0:C_W]))
    kf = _dot(h, w_ref[:, C_W:2 * C_W]) * (DK_C ** -0.5)
    k = _round_bf(kf)
    v = _round_bf(_dot(h, w_ref[:, 2 * C_W:3 * C_W]))
    og = jax.nn.sigmoid(_dot(h, w_ref[:, 3 * C_W:4 * C_W]))
    p = _dot(h, w_ref[:, 4 * C_W:4 * C_W + P_W])
    gz = _dot(h, wg_ref[...]) + bg_ref[...]
    lf = _log_sigmoid(gz)
    eye = (lax.broadcasted_iota(jnp.int32, (DK_C, DK_C), 0)
           == lax.broadcasted_iota(jnp.int32, (DK_C, DK_C), 1))

    for hd in range(H_C):
        lanes = slice(hd * DK_C, (hd + 1) * DK_C)
        li = gz[:, hd:hd + 1]
        lfh = lf[:, H_C + hd:H_C + hd + 1]
        m_prev = m_ref[:, hd:hd + 1]
        qh = q[:, lanes]
        kh = k[:, lanes]
        vh = v[:, lanes]
        n_old = n_ref[:, lanes]
        m_t = jnp.maximum(lfh + m_prev, li)
        sc = jnp.sum(qh * kh, axis=-1, keepdims=True) * jnp.exp(li - m_t)
        w_inter = jnp.exp(lfh + m_prev - m_t)
        qc = jnp.concatenate(
            [_dot(qh[i:i + 1, :].astype(BF), c_ref[i, hd].astype(BF)) for i in range(nb)], axis=0)
        num = _round_bf(sc) * vh + w_inter * qc
        den = sc + w_inter * jnp.sum(qh * _round_bf(n_old), axis=-1, keepdims=True)
        hout = num / jnp.maximum(jnp.abs(den), jnp.exp(-m_t))
        mix_ref[:, lanes] = (og[:, lanes] * hout).astype(BF)
        a = w_inter
        wg = jnp.exp(li - m_t)
        wgk = wg * kf[:, lanes]
        n_out[:, lanes] = a * n_old + _round_bf(wg) * kh
        m_out[:, hd:hd + 1] = m_t
        for i in range(nb):
            kd = jnp.where(eye, jnp.broadcast_to(wgk[i:i + 1, :], (DK_C, DK_C)), 0.0).astype(BF)
            vrep = jnp.broadcast_to(vh[i:i + 1, :], (DK_C, DK_C)).astype(BF)
            c_out[i, hd] = a[i:i + 1, :] * c_ref[i, hd] + _dot(kd, vrep)

    prev = pool_ref[...]
    row = lax.broadcasted_iota(jnp.int32, prev.shape, 1)
    for gi, win in enumerate(POOL_WINDOWS):
        lanes = slice(gi * POOL_G, (gi + 1) * POOL_G)
        tail = jnp.sum(jnp.where(row >= POOL_PREV - (win - 1), prev, 0.0)[:, :, lanes], axis=1)
        mean = (tail + p[:, lanes]) / pool_div[gi]
        y = _dot((mean - p[:, lanes]).astype(BF), wp_ref[gi]) * ps_ref[:, lanes]
        mix_ref[:, C_W + gi * POOL_G:C_W + (gi + 1) * POOL_G] = y.astype(BF)
    pool_out[:, 0:POOL_PREV - 1, :] = pool_ref[:, 1:POOL_PREV, :]
    for i in range(nb):
        pool_out[i, POOL_PREV - 1:POOL_PREV, :] = p[i:i + 1, :]


def _odd_sample(x, g, w_main, wg, bg, wp, ps, c, n, m, pool, *, pool_div, nb):
    nsmp = x.shape[0]
    rows = lambda w: pl.BlockSpec((nb, w), lambda i: (i, 0))
    return pl.pallas_call(
        functools.partial(_odd_sample_kernel, pool_div=pool_div),
        grid=(nsmp // nb,),
        in_specs=[rows(D_MODEL)] + [_const_spec(a.shape) for a in (g, w_main, wg, bg, wp, ps)]
                 + [pl.BlockSpec((nb, H_C, DK_C, DK_C), lambda i: (i, 0, 0, 0)),
                    rows(C_W), rows(H_C),
                    pl.BlockSpec((nb, POOL_PREV, P_W), lambda i: (i, 0, 0))],
        out_specs=(rows(C_W + P_W),
                   pl.BlockSpec((nb, H_C, DK_C, DK_C), lambda i: (i, 0, 0, 0)),
                   rows(C_W), rows(H_C),
                   pl.BlockSpec((nb, POOL_PREV, P_W), lambda i: (i, 0, 0))),
        out_shape=(jax.ShapeDtypeStruct((nsmp, C_W + P_W), BF),
                   jax.ShapeDtypeStruct((nsmp, H_C, DK_C, DK_C), F32),
                   jax.ShapeDtypeStruct((nsmp, C_W), F32),
                   jax.ShapeDtypeStruct((nsmp, H_C), F32),
                   jax.ShapeDtypeStruct((nsmp, POOL_PREV, P_W), F32)),
        compiler_params=_params(("arbitrary",)),
        name="odd_sample",
    )(x, g, w_main, wg, bg, wp, ps, c, n, m, pool)


def _pad_cols(w, n):
    return jnp.pad(w, ((0, 0), (0, n - w.shape[1])))


def _ffn_weights(w_up, conv_w, conv_b, w_down):
    d = w_up.shape[0]
    wug = w_up[:, :D_FF].reshape(d, N_FF, FF_CHUNK).transpose(1, 0, 2).astype(BF)
    wuu = w_up[:, D_FF:].reshape(d, N_FF, FF_CHUNK).transpose(1, 0, 2).astype(BF)
    cw = conv_w.reshape(3, N_FF, FF_CHUNK).transpose(1, 0, 2)
    cb = conv_b.reshape(N_FF, 1, FF_CHUNK)
    wd = w_down.reshape(N_FF, FF_CHUNK, w_down.shape[1]).astype(BF)
    return wug, wuu, cw, cb, wd


def _ffn_state(st):
    return st.transpose(0, 2, 1, 3).reshape(st.shape[0], 2, D_FF)


def kernel(x_prompt, x_sample, cache_k, cache_v, cache_logf, state_conv_a, state_mlstm_c, state_mlstm_n,
           state_mlstm_m, state_pool, state_ffn_conv, page_table, norm_mix, norm_ffn, norm_final, w_in_even,
           b_forget_even, conv_a, w_out_even, w_in_odd, b_igate_odd, b_fgate_odd, w_pool_odd, pool_scale_odd,
           w_out_odd, w_up, ffn_conv_w, ffn_conv_b, w_down):
    b, s, _ = x_prompt.shape
    nsmp = x_sample.shape[0]
    n_pages = page_table.shape[1]
    assert x_sample.shape[1] == 1 and norm_mix.shape[0] == 2
    tm = min(512, s)
    tq = min(256, s)
    assert s % tm == 0 and n_pages % PAGES_PER_STEP == 0

    xs = x_sample.reshape(nsmp, D_MODEL)
    row = lambda v: v.reshape(1, -1)
    g_final = row(norm_final)

    w = w_in_even[0]
    n_main = 3 * A_W + 3 * B_W
    w_main = w[:, :n_main].astype(BF)
    wf = _pad_cols(w[:, n_main:], LANES).astype(BF)
    wft = jnp.pad(w[:, n_main:].T, ((0, 2 * SUBLANES - H_B), (0, 0))).astype(BF)
    bfl = _pad_cols(row(b_forget_even[0]), LANES)
    bft = jnp.pad(b_forget_even[0].reshape(H_B, 1), ((0, 2 * SUBLANES - H_B), (0, 0)))
    g_mix = row(norm_mix[0])
    wo = w_out_even[0].astype(BF)
    ffn0 = _ffn_weights(w_up[0], ffn_conv_w[0], ffn_conv_b[0], w_down[0])

    wqt = w[:, 3 * A_W:3 * A_W + B_W].T.astype(BF)
    wvt = w[:, 3 * A_W + 2 * B_W:n_main].T.astype(BF)
    a_p, qt_p, k_p, v_p, kb_p, vt_p, lf_p, fc_p, fr_p, ca_p = _even_proj(
        x_prompt, g_mix, w_main, wqt, wvt, wf, wft, bfl, bft, conv_a[0], tm=tm, tk=tq)
    att_p = _fox_prompt(qt_p, kb_p, vt_p, fc_p, fr_p, tq=tq)
    xp, ff0_p = _mix_ffn(x_prompt, a_p, att_p, 0, wo, row(norm_ffn[0]), *ffn0, g_final, tm=tm, final_norm=False)

    seg = jnp.asarray((np.arange(B_W)[:, None] // HD_B == np.arange(LANES)[None, :]).astype(np.float32), dtype=BF)
    a_s, q_s, k_s, v_s, lf_s, snew_s, ca_s = _even_sample(
        xs, g_mix, w_main, wf, bfl, conv_a[0], state_conv_a[0].reshape(nsmp, 2 * A_W), seg)
    qsel = (q_s.reshape(nsmp, H_B, 1, HD_B) * jnp.eye(H_B, dtype=F32)[None, :, :, None]).astype(BF)
    n_phys = cache_k.shape[1]
    att_s = _paged_attention(
        page_table, qsel, lf_s.reshape(nsmp, H_B, 1), snew_s[:, :H_B].reshape(nsmp, H_B, 1),
        v_s.reshape(nsmp, H_B, HD_B),
        cache_k[0].reshape(n_phys * PAGE_SIZE * H_B, HD_B), cache_v[0].reshape(n_phys * PAGE_SIZE * H_B, HD_B),
        cache_logf[0].transpose(0, 2, 1))
    xs1, ff0_s = _mix_ffn_sample(xs, a_s, att_s.reshape(nsmp, B_W).astype(BF), wo, row(norm_ffn[0]), *ffn0,
                                 g_final, state_ffn_conv[0].reshape(nsmp, 2 * D_FF), final_norm=False)

    w = w_in_odd[0]
    n_qkvo = 4 * C_W
    w_main = jnp.concatenate([w[:, :n_qkvo], w[:, n_qkvo + 2 * H_C:]], axis=1).astype(BF)
    w_gate = w[:, n_qkvo:n_qkvo + 2 * H_C]
    wg = _pad_cols(w_gate, LANES).astype(BF)
    wgt = jnp.pad(w_gate.T, ((0, 2 * SUBLANES - 2 * H_C), (0, 0))).astype(BF)
    b_gate = jnp.concatenate([b_igate_odd[0], b_fgate_odd[0]])
    bg = _pad_cols(row(b_gate), LANES)
    bgt = jnp.pad(b_gate.reshape(2 * H_C, 1), ((0, 2 * SUBLANES - 2 * H_C), (0, 0)))
    g_mix = row(norm_mix[1])
    wp = w_pool_odd[0].astype(BF)
    ps = row(pool_scale_odd[0])
    wo = w_out_odd[0].astype(BF)
    ffn1 = _ffn_weights(w_up[1], ffn_conv_w[1], ffn_conv_b[1], w_down[1])

    mix_p, c_p, n_p, m_p, pool_p = _odd_prompt(xp, g_mix, w_main, wg, wgt, bg, bgt, wp, ps, tm=tm)
    y_p, ff1_p = _mix_ffn(xp, mix_p, mix_p, 1, wo, row(norm_ffn[1]), *ffn1, g_final, tm=tm, final_norm=True)

    pos0 = n_pages * PAGE_SIZE
    pool_div = tuple(float(min(win, pos0 + 1)) for win in POOL_WINDOWS)
    mix_s, c_s, n_s, m_s, pool_s = _odd_sample(
        xs1, g_mix, w_main, wg, bg, wp, ps, state_mlstm_c[0], state_mlstm_n[0].reshape(nsmp, C_W),
        state_mlstm_m[0], state_pool[0], pool_div=pool_div, nb=min(8, nsmp))
    y_s, ff1_s = _mix_ffn_sample(xs1, mix_s[:, :C_W], mix_s[:, C_W:], wo, row(norm_ffn[1]), *ffn1, g_final,
                                 state_ffn_conv[1].reshape(nsmp, 2 * D_FF), final_norm=True)

    heads = lambda z: z.reshape(z.shape[:-1] + (H_B, HD_B))
    return (
        y_p, y_s.reshape(nsmp, 1, D_MODEL),
        heads(k_p)[None], heads(k_s).reshape(1, nsmp, 1, H_B, HD_B),
        heads(v_p)[None], heads(v_s).reshape(1, nsmp, 1, H_B, HD_B),
        lf_p[None], lf_s.reshape(1, nsmp, 1, H_B),
        ca_p[None], ca_s.reshape(1, nsmp, 2, A_W),
        c_p[None], c_s[None],
        n_p[:, :H_C][None], n_s.reshape(1, nsmp, H_C, DK_C),
        m_p[:, :H_C, 0][None], m_s[None],
        pool_p[:, 1:][None], pool_s[None],
        jnp.stack([_ffn_state(ff0_p), _ffn_state(ff1_p)]),
        jnp.stack([ff0_s.reshape(nsmp, 2, D_FF), ff1_s.reshape(nsmp, 2, D_FF)]),
    )
```

```python
import functools

import numpy as np
import jax
import jax.numpy as jnp
from jax import lax
from jax.experimental import pallas as pl
from jax.experimental.pallas import tpu as pltpu

D_MODEL = 1024
A_W = 512
B_W = 512
H_B = 8
HD_B = 64
C_W = 512
H_C = 4
DK_C = 128
P_W = 512
POOL_WINDOWS = (2, 4, 8, 16)
POOL_G = 128
POOL_PREV = 15
D_FF = 2816
PAGE_SIZE = 128
CHUNK = 128
EPS = 1e-6

LANES = 128
SUBLANES = 8
FF_CHUNK = 256
N_FF = D_FF // FF_CHUNK
PAGES_PER_STEP = 8
VMEM_LIMIT = 56 * 1024 * 1024
NEG = -1e30

BF = jnp.bfloat16
F32 = jnp.float32


def _rms(x, g):
    return x * lax.rsqrt(jnp.mean(x * x, axis=-1, keepdims=True) + EPS) * g


def _dot(a, b):
    return jnp.dot(a, b, preferred_element_type=F32)


def _dot_nt(a, b):
    return lax.dot_general(a, b, (((1,), (1,)), ((), ())), preferred_element_type=F32)


def _split3(x):
    hi = x.astype(BF)
    r = x - hi.astype(F32)
    mid = r.astype(BF)
    lo = (r - mid.astype(F32)).astype(BF)
    return hi, mid, lo


def _tri_dot_left(tri, x):
    hi, mid, lo = _split3(x)
    return _dot(tri, hi) + _dot(tri, mid) + _dot(tri, lo)


def _tri_dot_right(x, tri):
    hi, mid, lo = _split3(x)
    return _dot(hi, tri) + _dot(mid, tri) + _dot(lo, tri)


def _log_sigmoid(x):
    return jnp.minimum(x, 0.0) - jnp.log1p(jnp.exp(-jnp.abs(x)))


def _tri(n, lower):
    r = lax.broadcasted_iota(jnp.int32, (n, n), 0)
    c = lax.broadcasted_iota(jnp.int32, (n, n), 1)
    keep = (c <= r) if lower else (r <= c)
    return jnp.where(keep, 1.0, 0.0).astype(BF)


def _round_bf(x):
    return x.astype(BF).astype(F32)


def _const_spec(shape):
    zeros = (0,) * len(shape)
    return pl.BlockSpec(shape, lambda *_: zeros, pipeline_mode=pl.Buffered(1))


def _params(sem):
    return pltpu.CompilerParams(dimension_semantics=sem, vmem_limit_bytes=VMEM_LIMIT)


def _even_proj_kernel(x_ref, g_ref, w_ref, wqt_ref, wvt_ref, wf_ref, wft_ref, bf_ref, bft_ref, cw_ref,
                      a_ref, qt_ref, k_ref, v_ref, kb_ref, vt_ref, lf_ref, fc_ref, fr_ref, cn_ref,
                      ext_ref, ccol_ref, crow_ref, *, tm, tk):
    t = pl.program_id(1)

    @pl.when(t == 0)
    def _():
        ext_ref[0:SUBLANES, :] = jnp.zeros((SUBLANES, A_W), F32)
        ccol_ref[...] = jnp.zeros_like(ccol_ref)
        crow_ref[...] = jnp.zeros_like(crow_ref)

    h = _rms(x_ref[...], g_ref[...]).astype(BF)
    u = _dot(h, w_ref[:, 0:A_W])
    gb = _dot(h, w_ref[:, A_W:2 * A_W])
    gc = _dot(h, w_ref[:, 2 * A_W:3 * A_W])
    cu = gc * u
    ext_ref[SUBLANES:SUBLANES + tm, :] = cu
    cw = cw_ref[...]
    conv = (cw[0:1] * ext_ref[SUBLANES - 2:SUBLANES - 2 + tm, :]
            + cw[1:2] * ext_ref[SUBLANES - 1:SUBLANES - 1 + tm, :]
            + cw[2:3] * cu)
    a_ref[...] = (gb * conv).astype(BF)
    cn_ref[...] = ext_ref[tm + SUBLANES - 2:tm + SUBLANES, :]
    ext_ref[0:SUBLANES, :] = ext_ref[tm:tm + SUBLANES, :]

    base = 3 * A_W
    k = _dot(h, w_ref[:, base + B_W:base + 2 * B_W])
    k_ref[...] = k
    kb_ref[...] = k.astype(BF)
    v_ref[...] = _dot(h, w_ref[:, base + 2 * B_W:base + 3 * B_W])
    qt = (_dot_nt(wqt_ref[...], h) * (HD_B ** -0.5)).astype(BF)
    vt = _dot_nt(wvt_ref[...], h).astype(BF)
    for j in range(tm // tk):
        qt_ref[j] = qt[:, j * tk:(j + 1) * tk]
        vt_ref[j] = vt[:, j * tk:(j + 1) * tk]

    lf = _log_sigmoid(_dot(h, wf_ref[...]) + bf_ref[...])
    lf_ref[...] = lf[:, 0:H_B]
    cs = _tri_dot_left(_tri(tm, True), lf) + ccol_ref[0:1, :]
    fc_ref[...] = cs[:, 0:H_B]
    ccol_ref[...] = jnp.broadcast_to(cs[tm - 1:tm, :], ccol_ref.shape)

    lft = _log_sigmoid(_dot_nt(wft_ref[...], h) + bft_ref[...])
    cst = _tri_dot_right(lft, _tri(tm, False)) + crow_ref[:, 0:1]
    for j in range(tm // tk):
        fr_ref[j] = cst[0:H_B, j * tk:(j + 1) * tk]
    crow_ref[...] = jnp.broadcast_to(cst[:, tm - 1:tm], crow_ref.shape)


def _even_proj(x, g, w_main, wqt, wvt, wf, wft, bfl, bft, cw, *, tm, tk):
    b, s, _ = x.shape
    nt = s // tm
    row = lambda w: pl.BlockSpec((None, tm, w), lambda i, t: (i, t, 0))
    chunked = lambda rows: pl.BlockSpec((None, tm // tk, rows, tk), lambda i, t: (i, t, 0, 0))
    out_shape = (
        jax.ShapeDtypeStruct((b, s, A_W), BF),
        jax.ShapeDtypeStruct((b, s // tk, B_W, tk), BF),
        jax.ShapeDtypeStruct((b, s, B_W), F32),
        jax.ShapeDtypeStruct((b, s, B_W), F32),
        jax.ShapeDtypeStruct((b, s, B_W), BF),
        jax.ShapeDtypeStruct((b, s // tk, B_W, tk), BF),
        jax.ShapeDtypeStruct((b, s, H_B), F32),
        jax.ShapeDtypeStruct((b, s, H_B), F32),
        jax.ShapeDtypeStruct((b, s // tk, H_B, tk), F32),
        jax.ShapeDtypeStruct((b, 2, A_W), F32),
    )
    out_specs = (
        row(A_W), chunked(B_W), row(B_W), row(B_W), row(B_W), chunked(B_W), row(H_B), row(H_B),
        chunked(H_B),
        pl.BlockSpec((None, 2, A_W), lambda i, t: (i, 0, 0)),
    )
    in_specs = [row(D_MODEL)] + [_const_spec(a.shape) for a in (g, w_main, wqt, wvt, wf, wft, bfl, bft, cw)]
    return pl.pallas_call(
        functools.partial(_even_proj_kernel, tm=tm, tk=tk),
        grid=(b, nt), in_specs=in_specs, out_specs=out_specs, out_shape=out_shape,
        scratch_shapes=[pltpu.VMEM((tm + SUBLANES, A_W), F32),
                        pltpu.VMEM((SUBLANES, LANES), F32),
                        pltpu.VMEM((2 * SUBLANES, LANES), F32)],
        compiler_params=_params(("arbitrary", "arbitrary")),
        name="even_proj",
    )(x, g, w_main, wqt, wvt, wf, wft, bfl, bft, cw)


def _fox_kernel(qt_ref, k_ref, vt_ref, fc_ref, fr_ref, o_ref, qm_scr, m_scr, l_scr, acc_scr, *, tq):
    qi = pl.program_id(1)
    sub = lax.broadcasted_iota(jnp.int32, (LANES, tq), 0)
    for h in range(H_B):
        hp, hh = divmod(h, 2)
        qp = qt_ref[hp * LANES:(hp + 1) * LANES, :]
        in_head = (sub < HD_B) if hh == 0 else (sub >= HD_B)
        qm_scr[h] = jnp.where(in_head, qp, jnp.zeros_like(qp))
    m_scr[...] = jnp.full(m_scr.shape, NEG, F32)
    l_scr[...] = jnp.zeros_like(l_scr)
    acc_scr[...] = jnp.zeros_like(acc_scr)
    causal = (lax.broadcasted_iota(jnp.int32, (tq, tq), 0) <= lax.broadcasted_iota(jnp.int32, (tq, tq), 1))

    def step(j, masked):
        off = pl.multiple_of(j * tq, tq)
        group = lambda h: slice((h // 2) * LANES, (h // 2 + 1) * LANES)
        scores = [_dot(k_ref[pl.ds(off, tq), group(h)], qm_scr[h]) for h in range(H_B)]
        probs, alphas = [], []
        for h in range(H_B):
            s = scores[h] + (fr_ref[h:h + 1, :] - fc_ref[pl.ds(off, tq), h:h + 1])
            if masked:
                s = jnp.where(causal, s, NEG)
            m_old = m_scr[h:h + 1, :]
            m_new = jnp.maximum(m_old, jnp.max(s, axis=0, keepdims=True))
            p = jnp.exp(s - m_new)
            alpha = jnp.exp(m_old - m_new)
            l_scr[h:h + 1, :] = alpha * l_scr[h:h + 1, :] + jnp.sum(p, axis=0, keepdims=True)
            m_scr[h:h + 1, :] = m_new
            probs.append(p.astype(BF))
            alphas.append(alpha)
        for h in range(H_B):
            hp, hh = divmod(h, 2)
            rows = slice(hh * HD_B, (hh + 1) * HD_B)
            pv = _dot(vt_ref[j, group(h), :], probs[h])
            acc_scr[hp, rows, :] = alphas[h] * acc_scr[hp, rows, :] + pv[rows, :]

    def body(j, carry):
        step(j, False)
        return carry

    lax.fori_loop(0, qi, body, 0)
    step(qi, True)
    for hp in range(H_B // 2):
        denom = jnp.where(sub < HD_B, l_scr[2 * hp:2 * hp + 1, :], l_scr[2 * hp + 1:2 * hp + 2, :])
        o_ref[:, hp * LANES:(hp + 1) * LANES] = (acc_scr[hp] / denom).T.astype(BF)


def _fox_prompt(qt, kb, vt, fc, fr, *, tq):
    b, nq, _, _ = qt.shape
    s = nq * tq
    return pl.pallas_call(
        functools.partial(_fox_kernel, tq=tq),
        grid=(b, nq),
        in_specs=[
            pl.BlockSpec((None, None, B_W, tq), lambda i, t: (i, t, 0, 0)),
            pl.BlockSpec((None, s, B_W), lambda i, t: (i, 0, 0)),
            pl.BlockSpec((None, nq, B_W, tq), lambda i, t: (i, 0, 0, 0)),
            pl.BlockSpec((None, s, H_B), lambda i, t: (i, 0, 0)),
            pl.BlockSpec((None, None, H_B, tq), lambda i, t: (i, t, 0, 0)),
        ],
        out_specs=pl.BlockSpec((None, tq, B_W), lambda i, t: (i, t, 0)),
        out_shape=jax.ShapeDtypeStruct((b, s, B_W), BF),
        scratch_shapes=[pltpu.VMEM((H_B, LANES, tq), BF), pltpu.VMEM((H_B, tq), F32),
                        pltpu.VMEM((H_B, tq), F32), pltpu.VMEM((H_B // 2, LANES, tq), F32)],
        compiler_params=_params(("arbitrary", "arbitrary")),
        name="fox_prompt",
    )(qt, kb, vt, fc, fr)


def _mix_ffn_kernel(x_ref, ma_ref, mb_ref, wo_ref, gf_ref, wug_ref, wuu_ref, cw_ref, cb_ref, wd_ref, gfin_ref,
                    y_ref, st_ref, acc_ref, h_ref, act_ref, ext_ref, carry_ref, *, tm, final_norm):
    t = pl.program_id(1)

    @pl.when(t == 0)
    def _():
        carry_ref[...] = jnp.zeros_like(carry_ref)

    half = wo_ref.shape[0] // 2
    x1 = x_ref[...] + _dot(ma_ref[...], wo_ref[0:half, :]) + _dot(mb_ref[...], wo_ref[half:2 * half, :])
    acc_ref[...] = x1
    h_ref[...] = _rms(x1, gf_ref[...]).astype(BF)

    def up(c):
        return _dot(h_ref[...], wug_ref[c]), _dot(h_ref[...], wuu_ref[c])

    nxt = up(0)
    for c in range(N_FF):
        g, u = nxt
        if c + 1 < N_FF:
            nxt = up(c + 1)
        ext = ext_ref.at[c % 2]
        ext[0:SUBLANES, :] = carry_ref[c]
        ext[SUBLANES:SUBLANES + tm, :] = g
        cw = cw_ref[c]
        gconv = (cw[0:1] * ext[SUBLANES - 2:SUBLANES - 2 + tm, :]
                 + cw[1:2] * ext[SUBLANES - 1:SUBLANES - 1 + tm, :]
                 + cw[2:3] * g + cb_ref[c])
        act_ref[:, c * FF_CHUNK:(c + 1) * FF_CHUNK] = (gconv * jax.nn.sigmoid(gconv) * u).astype(BF)
        carry_ref[c] = ext[tm:tm + SUBLANES, :]
        st_ref[c] = ext[tm + SUBLANES - 2:tm + SUBLANES, :]
    y = acc_ref[...] + _dot(act_ref[...], wd_ref[...])
    y_ref[...] = _rms(y, gfin_ref[...]) if final_norm else y


def _mix_ffn(x, ma, mb, lane_blk_b, wo, gf, wug, wuu, cw, cb, wd, gfin, *, tm, final_norm):
    b, s, _ = x.shape
    nt = s // tm
    half = wo.shape[0] // 2
    return pl.pallas_call(
        functools.partial(_mix_ffn_kernel, tm=tm, final_norm=final_norm),
        grid=(b, nt),
        in_specs=[
            pl.BlockSpec((None, tm, D_MODEL), lambda i, t: (i, t, 0)),
            pl.BlockSpec((None, tm, half), lambda i, t: (i, t, 0)),
            pl.BlockSpec((None, tm, half), lambda i, t: (i, t, lane_blk_b)),
            _const_spec(wo.shape), _const_spec(gf.shape), _const_spec(wug.shape), _const_spec(wuu.shape),
            _const_spec(cw.shape), _const_spec(cb.shape), _const_spec(wd.shape), _const_spec(gfin.shape),
        ],
        out_specs=(pl.BlockSpec((None, tm, D_MODEL), lambda i, t: (i, t, 0)),
                   pl.BlockSpec((None, N_FF, 2, FF_CHUNK), lambda i, t: (i, 0, 0, 0))),
        out_shape=(jax.ShapeDtypeStruct((b, s, D_MODEL), F32),
                   jax.ShapeDtypeStruct((b, N_FF, 2, FF_CHUNK), F32)),
        scratch_shapes=[pltpu.VMEM((tm, D_MODEL), F32), pltpu.VMEM((tm, D_MODEL), BF),
                        pltpu.VMEM((tm, D_FF), BF),
                        pltpu.VMEM((2, tm + SUBLANES, FF_CHUNK), F32),
                        pltpu.VMEM((N_FF, SUBLANES, FF_CHUNK), F32)],
        compiler_params=_params(("arbitrary", "arbitrary")),
        name="mix_ffn",
    )(x, ma, mb, wo, gf, wug, wuu, cw, cb, wd, gfin)


def _mix_ffn_sample_kernel(x_ref, ma_ref, mb_ref, wo_ref, gf_ref, wug_ref, wuu_ref, cw_ref, cb_ref, wd_ref,
                           gfin_ref, st_ref, y_ref, stn_ref, *, final_norm):
    half = wo_ref.shape[0] // 2
    x1 = x_ref[...] + _dot(ma_ref[...], wo_ref[0:half, :]) + _dot(mb_ref[...], wo_ref[half:2 * half, :])
    h = _rms(x1, gf_ref[...]).astype(BF)
    y_ref[...] = x1
    for c in range(N_FF):
        cols = slice(c * FF_CHUNK, (c + 1) * FF_CHUNK)
        cols1 = slice(D_FF + c * FF_CHUNK, D_FF + (c + 1) * FF_CHUNK)
        g = _dot(h, wug_ref[c])
        u = _dot(h, wuu_ref[c])
        cw = cw_ref[c]
        prev1 = st_ref[:, cols1]
        gconv = cw[0:1] * st_ref[:, cols] + cw[1:2] * prev1 + cw[2:3] * g + cb_ref[c]
        act = (gconv * jax.nn.sigmoid(gconv) * u).astype(BF)
        y_ref[...] += _dot(act, wd_ref[cols, :])
        stn_ref[:, cols] = prev1
        stn_ref[:, cols1] = g
    if final_norm:
        y_ref[...] = _rms(y_ref[...], gfin_ref[...])


def _mix_ffn_sample(x, ma, mb, wo, gf, wug, wuu, cw, cb, wd, gfin, st, *, final_norm):
    n = x.shape[0]
    return pl.pallas_call(
        functools.partial(_mix_ffn_sample_kernel, final_norm=final_norm),
        out_shape=(jax.ShapeDtypeStruct((n, D_MODEL), F32), jax.ShapeDtypeStruct((n, 2 * D_FF), F32)),
        compiler_params=pltpu.CompilerParams(vmem_limit_bytes=VMEM_LIMIT),
        name="mix_ffn_sample",
    )(x, ma, mb, wo, gf, wug, wuu, cw, cb, wd, gfin, st)


def _odd_kernel(x_ref, g_ref, w_ref, wg_ref, wgt_ref, bg_ref, bgt_ref, wp_ref, ps_ref,
                mix_ref, c_out, n_out, m_out, pool_out,
                c_scr, n_scr, m_scr, e_ref, s_a, s_b, hc_scr, *, tm):
    t = pl.program_id(1)
    top = 3 * SUBLANES

    @pl.when(t == 0)
    def _():
        c_scr[...] = jnp.zeros_like(c_scr)
        n_scr[...] = jnp.zeros_like(n_scr)
        m_scr[...] = jnp.zeros_like(m_scr)
        e_ref[0:top, :] = jnp.zeros((top, P_W), F32)
        s_a[0:SUBLANES, :] = jnp.zeros((SUBLANES, P_W), F32)
        s_b[0:SUBLANES, :] = jnp.zeros((SUBLANES, P_W), F32)

    h = _rms(x_ref[...], g_ref[...]).astype(BF)
    q = _dot(h, w_ref[:, 0:C_W]).astype(BF)
    kf = _dot(h, w_ref[:, C_W:2 * C_W]) * (DK_C ** -0.5)
    kb = kf.astype(BF)
    vb = _dot(h, w_ref[:, 2 * C_W:3 * C_W]).astype(BF)
    og = jax.nn.sigmoid(_dot(h, w_ref[:, 3 * C_W:4 * C_W]))
    p = _dot(h, w_ref[:, 4 * C_W:4 * C_W + P_W])

    gz = _dot(h, wg_ref[...]) + bg_ref[...]
    lf_c = _log_sigmoid(gz)
    gzt = _dot_nt(wgt_ref[...], h) + bgt_ref[...]
    lf_r = _log_sigmoid(gzt)

    tri_l = _tri(CHUNK, True)
    tri_u = _tri(CHUNK, False)
    r_i = lax.broadcasted_iota(jnp.int32, (CHUNK, CHUNK), 0)
    c_i = lax.broadcasted_iota(jnp.int32, (CHUNK, CHUNK), 1)
    causal = c_i <= r_i

    for ci in range(tm // CHUNK):
        rows = slice(ci * CHUNK, (ci + 1) * CHUNK)
        bcum_c = _tri_dot_left(tri_l, lf_c[rows, :])
        bcum_r = _tri_dot_right(lf_r[:, rows], tri_u)
        for hd in range(H_C):
            lanes = slice(hd * DK_C, (hd + 1) * DK_C)
            bc = bcum_c[:, H_C + hd:H_C + hd + 1]
            br = bcum_r[H_C + hd:H_C + hd + 1, :]
            li_c = gz[rows, hd:hd + 1]
            li_r = gzt[hd:hd + 1, rows]
            m_prev = m_scr[hd:hd + 1, 0:1]
            dlog = jnp.where(causal, bc + (li_r - br), NEG)
            inter = bc + m_prev
            m_t = jnp.maximum(inter, jnp.max(dlog, axis=-1, keepdims=True))
            qh = q[rows, lanes]
            kh = kb[rows, lanes]
            vh = vb[rows, lanes]
            sc = _dot_nt(qh, kh) * jnp.exp(dlog - m_t)
            w_inter = jnp.exp(inter - m_t)
            c_old = c_scr[hd]
            n_old = n_scr[hd:hd + 1, :]
            num = _dot(sc.astype(BF), vh) + w_inter * _dot(qh, c_old.astype(BF))
            qn = jnp.sum(qh.astype(F32) * _round_bf(n_old), axis=-1, keepdims=True)
            den = jnp.sum(sc, axis=-1, keepdims=True) + w_inter * qn
            hc_scr[rows, lanes] = num / jnp.maximum(jnp.abs(den), jnp.exp(-m_t))
            bl = bc[CHUNK - 1:CHUNK, :]
            g_c = bl - bc + li_c
            m_new = jnp.maximum(bl + m_prev, jnp.max(g_c, axis=0, keepdims=True))
            a = jnp.exp(bl + m_prev - m_new)
            wg = jnp.exp(g_c - m_new)
            wgk = wg * kf[rows, lanes]
            c_scr[hd] = a * c_old + _dot(wgk.T.astype(BF), vh)
            n_scr[hd:hd + 1, :] = a * n_old + jnp.sum(_round_bf(wg) * kh.astype(F32), axis=0, keepdims=True)
            m_scr[hd:hd + 1, :] = jnp.broadcast_to(m_new, (1, LANES))

    mix_ref[:, 0:C_W] = (og * hc_scr[...]).astype(BF)

    n_rows = tm + top
    e_ref[top:n_rows, :] = p
    lo = SUBLANES
    s_a[lo:n_rows, :] = e_ref[lo:n_rows, :] + e_ref[lo - 1:n_rows - 1, :]
    s_b[lo:n_rows, :] = s_a[lo:n_rows, :] + s_a[lo - 2:n_rows - 2, :]
    w2 = s_a[top:n_rows, 0:POOL_G]
    s_a[lo:n_rows, :] = s_b[lo:n_rows, :] + s_b[lo - 4:n_rows - 4, :]
    w4 = s_b[top:n_rows, POOL_G:2 * POOL_G]
    w8 = s_a[top:n_rows, 2 * POOL_G:3 * POOL_G]
    w16 = s_a[top:n_rows, 3 * POOL_G:4 * POOL_G] + s_a[top - 8:n_rows - 8, 3 * POOL_G:4 * POOL_G]
    pos1 = (t * tm + lax.broadcasted_iota(jnp.int32, (tm, 1), 0) + 1).astype(F32)
    for gi, (win, wsum) in enumerate(zip(POOL_WINDOWS, (w2, w4, w8, w16))):
        lanes = slice(gi * POOL_G, (gi + 1) * POOL_G)
        mean = wsum / jnp.minimum(float(win), pos1)
        y = _dot((mean - p[:, lanes]).astype(BF), wp_ref[gi]) * ps_ref[:, lanes]
        mix_ref[:, C_W + gi * POOL_G:C_W + (gi + 1) * POOL_G] = y.astype(BF)
    pool_out[...] = e_ref[n_rows - 2 * SUBLANES:n_rows, :]
    e_ref[SUBLANES:top, :] = e_ref[n_rows - 2 * SUBLANES:n_rows, :]

    c_out[...] = c_scr[...]
    n_out[...] = n_scr[...]
    m_out[...] = m_scr[...]


def _odd_prompt(x, g, w_main, wg, wgt, bg, bgt, wp, ps, *, tm):
    b, s, _ = x.shape
    nt = s // tm
    state = lambda *shape: pl.BlockSpec((None,) + shape, lambda i, t: (i,) + (0,) * len(shape))
    return pl.pallas_call(
        functools.partial(_odd_kernel, tm=tm),
        grid=(b, nt),
        in_specs=[pl.BlockSpec((None, tm, D_MODEL), lambda i, t: (i, t, 0))]
                 + [_const_spec(a.shape) for a in (g, w_main, wg, wgt, bg, bgt, wp, ps)],
        out_specs=(pl.BlockSpec((None, tm, C_W + P_W), lambda i, t: (i, t, 0)),
                   state(H_C, DK_C, DK_C), state(SUBLANES, DK_C), state(SUBLANES, LANES),
                   state(2 * SUBLANES, P_W)),
        out_shape=(jax.ShapeDtypeStruct((b, s, C_W + P_W), BF),
                   jax.ShapeDtypeStruct((b, H_C, DK_C, DK_C), F32),
                   jax.ShapeDtypeStruct((b, SUBLANES, DK_C), F32),
                   jax.ShapeDtypeStruct((b, SUBLANES, LANES), F32),
                   jax.ShapeDtypeStruct((b, 2 * SUBLANES, P_W), F32)),
        scratch_shapes=[pltpu.VMEM((H_C, DK_C, DK_C), F32), pltpu.VMEM((SUBLANES, DK_C), F32),
                        pltpu.VMEM((SUBLANES, LANES), F32),
                        pltpu.VMEM((tm + 3 * SUBLANES, P_W), F32),
                        pltpu.VMEM((tm + 3 * SUBLANES, P_W), F32),
                        pltpu.VMEM((tm + 3 * SUBLANES, P_W), F32),
                        pltpu.VMEM((tm, C_W), F32)],
        compiler_params=_params(("arbitrary", "arbitrary")),
        name="odd_prompt",
    )(x, g, w_main, wg, wgt, bg, bgt, wp, ps)


def _even_sample_kernel(x_ref, g_ref, w_ref, wf_ref, bf_ref, cw_ref, st_ref, seg_ref,
                        a_ref, q_ref, k_ref, v_ref, lf_ref, snew_ref, cn_ref):
    h = _rms(x_ref[...], g_ref[...]).astype(BF)
    u = _dot(h, w_ref[:, 0:A_W])
    gb = _dot(h, w_ref[:, A_W:2 * A_W])
    gc = _dot(h, w_ref[:, 2 * A_W:3 * A_W])
    cu = gc * u
    cw = cw_ref[...]
    prev1 = st_ref[:, A_W:2 * A_W]
    a_ref[...] = (gb * (cw[0:1] * st_ref[:, 0:A_W] + cw[1:2] * prev1 + cw[2:3] * cu)).astype(BF)
    cn_ref[:, 0:A_W] = prev1
    cn_ref[:, A_W:2 * A_W] = cu
    base = 3 * A_W
    qs = _round_bf(_dot(h, w_ref[:, base:base + B_W]) * (HD_B ** -0.5))
    k = _dot(h, w_ref[:, base + B_W:base + 2 * B_W])
    v = _dot(h, w_ref[:, base + 2 * B_W:base + 3 * B_W])
    k_ref[...] = k
    v_ref[...] = v
    lf_ref[...] = _log_sigmoid(_dot(h, wf_ref[...]) + bf_ref[...])[:, 0:H_B]
    snew_ref[...] = _tri_dot_right(qs * _round_bf(k), seg_ref[...])
    q_ref[...] = qs


def _even_sample(x, g, w_main, wf, bfl, cw, st, seg):
    n = x.shape[0]
    return pl.pallas_call(
        _even_sample_kernel,
        out_shape=(jax.ShapeDtypeStruct((n, A_W), BF),
                   jax.ShapeDtypeStruct((n, B_W), F32),
                   jax.ShapeDtypeStruct((n, B_W), F32),
                   jax.ShapeDtypeStruct((n, B_W), F32),
                   jax.ShapeDtypeStruct((n, H_B), F32),
                   jax.ShapeDtypeStruct((n, LANES), F32),
                   jax.ShapeDtypeStruct((n, 2 * A_W), F32)),
        compiler_params=pltpu.CompilerParams(vmem_limit_bytes=VMEM_LIMIT),
        name="even_sample",
    )(x, g, w_main, wf, bfl, cw, st, seg)


def _paged_kernel(pt_ref, qsel_ref, lfn_ref, snew_ref, vnew_ref, *refs, n_steps):
    npg = PAGES_PER_STEP
    k_refs = refs[0:npg]
    v_refs = refs[npg:2 * npg]
    f_refs = refs[2 * npg:3 * npg]
    o_ref = refs[3 * npg]
    m_scr, l_scr, acc_scr, suf_scr = refs[3 * npg + 1:]
    c = pl.program_id(1)
    head_row = lax.broadcasted_iota(jnp.int32, (H_B, PAGE_SIZE), 0)
    head_row_v = lax.broadcasted_iota(jnp.int32, (H_B, HD_B), 0)

    def head_t(ref, h):
        return ref[h].astype(BF)

    @pl.when(c == 0)
    def _():
        m_scr[...] = jnp.broadcast_to(snew_ref[...], m_scr.shape)
        l_scr[...] = jnp.ones_like(l_scr)
        acc_scr[...] = _round_bf(vnew_ref[...])
        suf_scr[...] = jnp.broadcast_to(lfn_ref[...], suf_scr.shape)

    strict = _tri(PAGE_SIZE, True) - jnp.where(
        lax.broadcasted_iota(jnp.int32, (PAGE_SIZE, PAGE_SIZE), 0)
        == lax.broadcasted_iota(jnp.int32, (PAGE_SIZE, PAGE_SIZE), 1), 1.0, 0.0).astype(BF)
    ones = jnp.ones((PAGE_SIZE, LANES), BF)
    suf = suf_scr[:, 0:1]
    scores = []
    for j in reversed(range(npg)):
        lf = f_refs[j][...]
        within = _tri_dot_right(lf, strict)
        qk = _dot(qsel_ref[0], head_t(k_refs[j], 0))
        for h in range(1, H_B):
            qk = jnp.where(head_row == h, _dot(qsel_ref[h], head_t(k_refs[j], h)), qk)
        scores.append(qk + (within + suf))
        suf = suf + _tri_dot_right(lf, ones)[:, 0:1]
    suf_scr[...] = jnp.broadcast_to(suf, suf_scr.shape)
    m_old = m_scr[:, 0:1]
    m_new = m_old
    for s in scores:
        m_new = jnp.maximum(m_new, jnp.max(s, axis=-1, keepdims=True))
    alpha = jnp.exp(m_old - m_new)
    l = alpha * l_scr[:, 0:1]
    acc = alpha * acc_scr[...]
    for s, j in zip(scores, reversed(range(npg))):
        p = jnp.exp(s - m_new)
        l = l + jnp.sum(p, axis=-1, keepdims=True)
        pv = _dot_nt(jnp.where(head_row == 0, p, 0.0).astype(BF), head_t(v_refs[j], 0))
        for h in range(1, H_B):
            ph = jnp.where(head_row == h, p, 0.0).astype(BF)
            pv = jnp.where(head_row_v == h, _dot_nt(ph, head_t(v_refs[j], h)), pv)
        acc = acc + pv
    acc_scr[...] = acc
    m_scr[...] = jnp.broadcast_to(m_new, m_scr.shape)
    l_scr[...] = jnp.broadcast_to(l, l_scr.shape)

    @pl.when(c == n_steps - 1)
    def _():
        o_ref[...] = acc / l


def _paged_attention(page_table, qsel, lfn, snew, vnew, cache_k, cache_v, cache_lft):
    n, n_pages = page_table.shape
    npg = PAGES_PER_STEP
    n_steps = n_pages // npg

    def page_of(i, c, pt, j):
        return pt[i, (n_steps - 1 - c) * npg + j]

    per_sample = lambda *shape: pl.BlockSpec((None,) + shape, lambda i, c, pt: (i,) + (0,) * len(shape))
    kv_spec = lambda j: pl.BlockSpec((None, H_B, HD_B, PAGE_SIZE),
                                     lambda i, c, pt: (page_of(i, c, pt, j), 0, 0, 0))
    lf_spec = lambda j: pl.BlockSpec((None, H_B, PAGE_SIZE), lambda i, c, pt: (page_of(i, c, pt, j), 0, 0))
    in_specs = ([per_sample(H_B, H_B, HD_B), per_sample(H_B, 1), per_sample(H_B, 1), per_sample(H_B, HD_B)]
                + [kv_spec(j) for j in range(npg)] + [kv_spec(j) for j in range(npg)]
                + [lf_spec(j) for j in range(npg)])
    grid_spec = pltpu.PrefetchScalarGridSpec(
        num_scalar_prefetch=1, grid=(n, n_steps), in_specs=in_specs,
        out_specs=per_sample(H_B, HD_B),
        scratch_shapes=[pltpu.VMEM((H_B, LANES), F32), pltpu.VMEM((H_B, LANES), F32),
                        pltpu.VMEM((H_B, HD_B), F32), pltpu.VMEM((H_B, LANES), F32)])
    return pl.pallas_call(
        functools.partial(_paged_kernel, n_steps=n_steps),
        grid_spec=grid_spec,
        out_shape=jax.ShapeDtypeStruct((n, H_B, HD_B), F32),
        compiler_params=_params(("arbitrary", "arbitrary")),
        name="paged_attention",
    )(page_table, qsel, lfn, snew, vnew, *([cache_k] * npg), *([cache_v] * npg), *([cache_lft] * npg))


def _odd_sample_kernel(x_ref, g_ref, w_ref, wg_ref, bg_ref, wp_ref, ps_ref, c_ref, n_ref, m_ref, pool_ref,
                       mix_ref, c_out, n_out, m_out, pool_out, *, pool_div):
    nb = x_ref.shape[0]
    h = _rms(x_ref[...], g_ref[...]).astype(BF)
    q = _round_bf(_dot(h, w_ref[:, 0:C_W]))
    kf = _dot(h, w_ref[:, C_W:2 * C_W]) * (DK_C ** -0.5)
    k = _round_bf(kf)
    v = _round_bf(_dot(h, w_ref[:, 2 * C_W:3 * C_W]))
    og = jax.nn.sigmoid(_dot(h, w_ref[:, 3 * C_W:4 * C_W]))
    p = _dot(h, w_ref[:, 4 * C_W:4 * C_W + P_W])
    gz = _dot(h, wg_ref[...]) + bg_ref[...]
    lf = _log_sigmoid(gz)
    eye = (lax.broadcasted_iota(jnp.int32, (DK_C, DK_C), 0)
           == lax.broadcasted_iota(jnp.int32, (DK_C, DK_C), 1))

    for hd in range(H_C):
        lanes = slice(hd * DK_C, (hd + 1) * DK_C)
        li = gz[:, hd:hd + 1]
        lfh = lf[:, H_C + hd:H_C + hd + 1]
        m_prev = m_ref[:, hd:hd + 1]
        qh = q[:, lanes]
        kh = k[:, lanes]
        vh = v[:, lanes]
        n_old = n_ref[:, lanes]
        m_t = jnp.maximum(lfh + m_prev, li)
        sc = jnp.sum(qh * kh, axis=-1, keepdims=True) * jnp.exp(li - m_t)
        w_inter = jnp.exp(lfh + m_prev - m_t)
        qc = jnp.concatenate(
            [_dot(qh[i:i + 1, :].astype(BF), c_ref[i, hd].astype(BF)) for i in range(nb)], axis=0)
        num = _round_bf(sc) * vh + w_inter * qc
        den = sc + w_inter * jnp.sum(qh * _round_bf(n_old), axis=-1, keepdims=True)
        hout = num / jnp.maximum(jnp.abs(den), jnp.exp(-m_t))
        mix_ref[:, lanes] = (og[:, lanes] * hout).astype(BF)
        a = w_inter
        wg = jnp.exp(li - m_t)
        wgk = wg * kf[:, lanes]
        n_out[:, lanes] = a * n_old + _round_bf(wg) * kh
        m_out[:, hd:hd + 1] = m_t
        for i in range(nb):
            kd = jnp.where(eye, jnp.broadcast_to(wgk[i:i + 1, :], (DK_C, DK_C)), 0.0).astype(BF)
            vrep = jnp.broadcast_to(vh[i:i + 1, :], (DK_C, DK_C)).astype(BF)
            c_out[i, hd] = a[i:i + 1, :] * c_ref[i, hd] + _dot(kd, vrep)

    prev = pool_ref[...]
    row = lax.broadcasted_iota(jnp.int32, prev.shape, 1)
    for gi, win in enumerate(POOL_WINDOWS):
        lanes = slice(gi * POOL_G, (gi + 1) * POOL_G)
        tail = jnp.sum(jnp.where(row >= POOL_PREV - (win - 1), prev, 0.0)[:, :, lanes], axis=1)
        mean = (tail + p[:, lanes]) / pool_div[gi]
        y = _dot((mean - p[:, lanes]).astype(BF), wp_ref[gi]) * ps_ref[:, lanes]
        mix_ref[:, C_W + gi * POOL_G:C_W + (gi + 1) * POOL_G] = y.astype(BF)
    pool_out[:, 0:POOL_PREV - 1, :] = pool_ref[:, 1:POOL_PREV, :]
    for i in range(nb):
        pool_out[i, POOL_PREV - 1:POOL_PREV, :] = p[i:i + 1, :]


def _odd_sample(x, g, w_main, wg, bg, wp, ps, c, n, m, pool, *, pool_div, nb):
    nsmp = x.shape[0]
    rows = lambda w: pl.BlockSpec((nb, w), lambda i: (i, 0))
    return pl.pallas_call(
        functools.partial(_odd_sample_kernel, pool_div=pool_div),
        grid=(nsmp // nb,),
        in_specs=[rows(D_MODEL)] + [_const_spec(a.shape) for a in (g, w_main, wg, bg, wp, ps)]
                 + [pl.BlockSpec((nb, H_C, DK_C, DK_C), lambda i: (i, 0, 0, 0)),
                    rows(C_W), rows(H_C),
                    pl.BlockSpec((nb, POOL_PREV, P_W), lambda i: (i, 0, 0))],
        out_specs=(rows(C_W + P_W),
                   pl.BlockSpec((nb, H_C, DK_C, DK_C), lambda i: (i, 0, 0, 0)),
                   rows(C_W), rows(H_C),
                   pl.BlockSpec((nb, POOL_PREV, P_W), lambda i: (i, 0, 0))),
        out_shape=(jax.ShapeDtypeStruct((nsmp, C_W + P_W), BF),
                   jax.ShapeDtypeStruct((nsmp, H_C, DK_C, DK_C), F32),
                   jax.ShapeDtypeStruct((nsmp, C_W), F32),
                   jax.ShapeDtypeStruct((nsmp, H_C), F32),
                   jax.ShapeDtypeStruct((nsmp, POOL_PREV, P_W), F32)),
        compiler_params=_params(("arbitrary",)),
        name="odd_sample",
    )(x, g, w_main, wg, bg, wp, ps, c, n, m, pool)


def _pad_cols(w, n):
    return jnp.pad(w, ((0, 0), (0, n - w.shape[1])))


def _ffn_weights(w_up, conv_w, conv_b, w_down):
    d = w_up.shape[0]
    wug = w_up[:, :D_FF].reshape(d, N_FF, FF_CHUNK).transpose(1, 0, 2).astype(BF)
    wuu = w_up[:, D_FF:].reshape(d, N_FF, FF_CHUNK).transpose(1, 0, 2).astype(BF)
    cw = conv_w.reshape(3, N_FF, FF_CHUNK).transpose(1, 0, 2)
    cb = conv_b.reshape(N_FF, 1, FF_CHUNK)
    wd = w_down.astype(BF)
    return wug, wuu, cw, cb, wd


def _ffn_state(st):
    return st.transpose(0, 2, 1, 3).reshape(st.shape[0], 2, D_FF)


def kernel(x_prompt, x_sample, cache_k, cache_v, cache_logf, state_conv_a, state_mlstm_c, state_mlstm_n,
           state_mlstm_m, state_pool, state_ffn_conv, page_table, norm_mix, norm_ffn, norm_final, w_in_even,
           b_forget_even, conv_a, w_out_even, w_in_odd, b_igate_odd, b_fgate_odd, w_pool_odd, pool_scale_odd,
           w_out_odd, w_up, ffn_conv_w, ffn_conv_b, w_down):
    b, s, _ = x_prompt.shape
    nsmp = x_sample.shape[0]
    n_pages = page_table.shape[1]
    assert x_sample.shape[1] == 1 and norm_mix.shape[0] == 2
    tm = min(512, s)
    tq = min(256, s)
    assert s % tm == 0 and n_pages % PAGES_PER_STEP == 0

    xs = x_sample.reshape(nsmp, D_MODEL)
    row = lambda v: v.reshape(1, -1)
    g_final = row(norm_final)

    w = w_in_even[0]
    n_main = 3 * A_W + 3 * B_W
    w_main = w[:, :n_main].astype(BF)
    wf = _pad_cols(w[:, n_main:], LANES).astype(BF)
    wft = jnp.pad(w[:, n_main:].T, ((0, 2 * SUBLANES - H_B), (0, 0))).astype(BF)
    bfl = _pad_cols(row(b_forget_even[0]), LANES)
    bft = jnp.pad(b_forget_even[0].reshape(H_B, 1), ((0, 2 * SUBLANES - H_B), (0, 0)))
    g_mix = row(norm_mix[0])
    wo = w_out_even[0].astype(BF)
    ffn0 = _ffn_weights(w_up[0], ffn_conv_w[0], ffn_conv_b[0], w_down[0])

    wqt = w[:, 3 * A_W:3 * A_W + B_W].T.astype(BF)
    wvt = w[:, 3 * A_W + 2 * B_W:n_main].T.astype(BF)
    a_p, qt_p, k_p, v_p, kb_p, vt_p, lf_p, fc_p, fr_p, ca_p = _even_proj(
        x_prompt, g_mix, w_main, wqt, wvt, wf, wft, bfl, bft, conv_a[0], tm=tm, tk=tq)
    att_p = _fox_prompt(qt_p, kb_p, vt_p, fc_p, fr_p, tq=tq)
    xp, ff0_p = _mix_ffn(x_prompt, a_p, att_p, 0, wo, row(norm_ffn[0]), *ffn0, g_final, tm=tm, final_norm=False)

    seg = jnp.asarray((np.arange(B_W)[:, None] // HD_B == np.arange(LANES)[None, :]).astype(np.float32), dtype=BF)
    a_s, q_s, k_s, v_s, lf_s, snew_s, ca_s = _even_sample(
        xs, g_mix, w_main, wf, bfl, conv_a[0], state_conv_a[0].reshape(nsmp, 2 * A_W), seg)
    qsel = (q_s.reshape(nsmp, H_B, 1, HD_B) * jnp.eye(H_B, dtype=F32)[None, :, :, None]).astype(BF)
    n_phys = cache_k.shape[1]
    att_s = _paged_attention(
        page_table, qsel, lf_s.reshape(nsmp, H_B, 1), snew_s[:, :H_B].reshape(nsmp, H_B, 1),
        v_s.reshape(nsmp, H_B, HD_B),
        cache_k[0].transpose(0, 2, 3, 1), cache_v[0].transpose(0, 2, 3, 1), cache_logf[0].transpose(0, 2, 1))
    xs1, ff0_s = _mix_ffn_sample(xs, a_s, att_s.reshape(nsmp, B_W).astype(BF), wo, row(norm_ffn[0]), *ffn0,
                                 g_final, state_ffn_conv[0].reshape(nsmp, 2 * D_FF), final_norm=False)

    w = w_in_odd[0]
    n_qkvo = 4 * C_W
    w_main = jnp.concatenate([w[:, :n_qkvo], w[:, n_qkvo + 2 * H_C:]], axis=1).astype(BF)
    w_gate = w[:, n_qkvo:n_qkvo + 2 * H_C]
    wg = _pad_cols(w_gate, LANES).astype(BF)
    wgt = jnp.pad(w_gate.T, ((0, 2 * SUBLANES - 2 * H_C), (0, 0))).astype(BF)
    b_gate = jnp.concatenate([b_igate_odd[0], b_fgate_odd[0]])
    bg = _pad_cols(row(b_gate), LANES)
    bgt = jnp.pad(b_gate.reshape(2 * H_C, 1), ((0, 2 * SUBLANES - 2 * H_C), (0, 0)))
    g_mix = row(norm_mix[1])
    wp = w_pool_odd[0].astype(BF)
    ps = row(pool_scale_odd[0])
    wo = w_out_odd[0].astype(BF)
    ffn1 = _ffn_weights(w_up[1], ffn_conv_w[1], ffn_conv_b[1], w_down[1])

    mix_p, c_p, n_p, m_p, pool_p = _odd_prompt(xp, g_mix, w_main, wg, wgt, bg, bgt, wp, ps, tm=tm)
    y_p, ff1_p = _mix_ffn(xp, mix_p, mix_p, 1, wo, row(norm_ffn[1]), *ffn1, g_final, tm=tm, final_norm=True)

    pos0 = n_pages * PAGE_SIZE
    pool_div = tuple(float(min(win, pos0 + 1)) for win in POOL_WINDOWS)
    mix_s, c_s, n_s, m_s, pool_s = _odd_sample(
        xs1, g_mix, w_main, wg, bg, wp, ps, state_mlstm_c[0], state_mlstm_n[0].reshape(nsmp, C_W),
        state_mlstm_m[0], state_pool[0], pool_div=pool_div, nb=min(8, nsmp))
    y_s, ff1_s = _mix_ffn_sample(xs1, mix_s[:, :C_W], mix_s[:, C_W:], wo, row(norm_ffn[1]), *ffn1, g_final,
                                 state_ffn_conv[1].reshape(nsmp, 2 * D_FF), final_norm=True)

    heads = lambda z: z.reshape(z.shape[:-1] + (H_B, HD_B))
    return (
        y_p, y_s.reshape(nsmp, 1, D_MODEL),
        heads(k_p)[None], heads(k_s).reshape(1, nsmp, 1, H_B, HD_B),
        heads(v_p)[None], heads(v_s).reshape(1, nsmp, 1, H_B, HD_B),
        lf_p[None], lf_s.reshape(1, nsmp, 1, H_B),
        ca_p[None], ca_s.reshape(1, nsmp, 2, A_W),
        c_p[None], c_s[None],
        n_p[:, :H_C][None], n_s.reshape(1, nsmp, H_C, DK_C),
        m_p[:, :H_C, 0][None], m_s[None],
        pool_p[:, 1:][None], pool_s[None],
        jnp.stack([_ffn_state(ff0_p), _ffn_state(ff1_p)]),
        jnp.stack([ff0_s.reshape(nsmp, 2, D_FF), ff1_s.reshape(nsmp, 2, D_FF)]),
    )
```

```python
import functools

import numpy as np
import jax
import jax.numpy as jnp
from jax import lax
from jax.experimental import pallas as pl
from jax.experimental.pallas import tpu as pltpu

D_MODEL = 1024
A_W = 512
B_W = 512
H_B = 8
HD_B = 64
C_W = 512
H_C = 4
DK_C = 128
P_W = 512
POOL_WINDOWS = (2, 4, 8, 16)
POOL_G = 128
POOL_PREV = 15
D_FF = 2816
PAGE_SIZE = 128
CHUNK = 128
EPS = 1e-6

LANES = 128
SUBLANES = 8
FF_CHUNK = 256
N_FF = D_FF // FF_CHUNK
PAGES_PER_STEP = 16
VMEM_LIMIT = 56 * 1024 * 1024
NEG = -1e30

BF = jnp.bfloat16
F32 = jnp.float32


def _rms(x, g):
    return x * lax.rsqrt(jnp.mean(x * x, axis=-1, keepdims=True) + EPS) * g


def _dot(a, b):
    return jnp.dot(a, b, preferred_element_type=F32)


def _dot_nt(a, b):
    return lax.dot_general(a, b, (((1,), (1,)), ((), ())), preferred_element_type=F32)


def _split3(x):
    hi = x.astype(BF)
    r = x - hi.astype(F32)
    mid = r.astype(BF)
    lo = (r - mid.astype(F32)).astype(BF)
    return hi, mid, lo


def _tri_dot_left(tri, x):
    hi, mid, lo = _split3(x)
    return _dot(tri, hi) + _dot(tri, mid) + _dot(tri, lo)


def _tri_dot_right(x, tri):
    hi, mid, lo = _split3(x)
    return _dot(hi, tri) + _dot(mid, tri) + _dot(lo, tri)


def _log_sigmoid(x):
    return jnp.minimum(x, 0.0) - jnp.log1p(jnp.exp(-jnp.abs(x)))


def _tri(n, lower):
    r = lax.broadcasted_iota(jnp.int32, (n, n), 0)
    c = lax.broadcasted_iota(jnp.int32, (n, n), 1)
    keep = (c <= r) if lower else (r <= c)
    return jnp.where(keep, 1.0, 0.0).astype(BF)


def _round_bf(x):
    return x.astype(BF).astype(F32)


def _const_spec(shape):
    zeros = (0,) * len(shape)
    return pl.BlockSpec(shape, lambda *_: zeros, pipeline_mode=pl.Buffered(1))


def _params(sem):
    return pltpu.CompilerParams(dimension_semantics=sem, vmem_limit_bytes=VMEM_LIMIT)


def _even_proj_kernel(x_ref, g_ref, w_ref, wqt_ref, wvt_ref, wf_ref, wft_ref, bf_ref, bft_ref, cw_ref,
                      a_ref, qt_ref, k_ref, v_ref, kb_ref, vt_ref, lf_ref, fc_ref, fr_ref, cn_ref,
                      ext_ref, ccol_ref, crow_ref, *, tm, tk):
    t = pl.program_id(1)

    @pl.when(t == 0)
    def _():
        ext_ref[0:SUBLANES, :] = jnp.zeros((SUBLANES, A_W), F32)
        ccol_ref[...] = jnp.zeros_like(ccol_ref)
        crow_ref[...] = jnp.zeros_like(crow_ref)

    h = _rms(x_ref[...], g_ref[...]).astype(BF)
    u = _dot(h, w_ref[:, 0:A_W])
    gb = _dot(h, w_ref[:, A_W:2 * A_W])
    gc = _dot(h, w_ref[:, 2 * A_W:3 * A_W])
    cu = gc * u
    ext_ref[SUBLANES:SUBLANES + tm, :] = cu
    cw = cw_ref[...]
    conv = (cw[0:1] * ext_ref[SUBLANES - 2:SUBLANES - 2 + tm, :]
            + cw[1:2] * ext_ref[SUBLANES - 1:SUBLANES - 1 + tm, :]
            + cw[2:3] * cu)
    a_ref[...] = (gb * conv).astype(BF)
    cn_ref[...] = ext_ref[tm + SUBLANES - 2:tm + SUBLANES, :]
    ext_ref[0:SUBLANES, :] = ext_ref[tm:tm + SUBLANES, :]

    base = 3 * A_W
    k = _dot(h, w_ref[:, base + B_W:base + 2 * B_W])
    k_ref[...] = k
    kb_ref[...] = k.astype(BF)
    v_ref[...] = _dot(h, w_ref[:, base + 2 * B_W:base + 3 * B_W])
    qt = (_dot_nt(wqt_ref[...], h) * (HD_B ** -0.5)).astype(BF)
    vt = _dot_nt(wvt_ref[...], h).astype(BF)
    for j in range(tm // tk):
        qt_ref[j] = qt[:, j * tk:(j + 1) * tk]
        vt_ref[j] = vt[:, j * tk:(j + 1) * tk]

    lf = _log_sigmoid(_dot(h, wf_ref[...]) + bf_ref[...])
    lf_ref[...] = lf[:, 0:H_B]
    cs = _tri_dot_left(_tri(tm, True), lf) + ccol_ref[0:1, :]
    fc_ref[...] = cs[:, 0:H_B]
    ccol_ref[...] = jnp.broadcast_to(cs[tm - 1:tm, :], ccol_ref.shape)

    lft = _log_sigmoid(_dot_nt(wft_ref[...], h) + bft_ref[...])
    cst = _tri_dot_right(lft, _tri(tm, False)) + crow_ref[:, 0:1]
    for j in range(tm // tk):
        fr_ref[j] = cst[0:H_B, j * tk:(j + 1) * tk]
    crow_ref[...] = jnp.broadcast_to(cst[:, tm - 1:tm], crow_ref.shape)


def _even_proj(x, g, w_main, wqt, wvt, wf, wft, bfl, bft, cw, *, tm, tk):
    b, s, _ = x.shape
    nt = s // tm
    row = lambda w: pl.BlockSpec((None, tm, w), lambda i, t: (i, t, 0))
    chunked = lambda rows: pl.BlockSpec((None, tm // tk, rows, tk), lambda i, t: (i, t, 0, 0))
    out_shape = (
        jax.ShapeDtypeStruct((b, s, A_W), BF),
        jax.ShapeDtypeStruct((b, s // tk, B_W, tk), BF),
        jax.ShapeDtypeStruct((b, s, B_W), F32),
        jax.ShapeDtypeStruct((b, s, B_W), F32),
        jax.ShapeDtypeStruct((b, s, B_W), BF),
        jax.ShapeDtypeStruct((b, s // tk, B_W, tk), BF),
        jax.ShapeDtypeStruct((b, s, H_B), F32),
        jax.ShapeDtypeStruct((b, s, H_B), F32),
        jax.ShapeDtypeStruct((b, s // tk, H_B, tk), F32),
        jax.ShapeDtypeStruct((b, 2, A_W), F32),
    )
    out_specs = (
        row(A_W), chunked(B_W), row(B_W), row(B_W), row(B_W), chunked(B_W), row(H_B), row(H_B),
        chunked(H_B),
        pl.BlockSpec((None, 2, A_W), lambda i, t: (i, 0, 0)),
    )
    in_specs = [row(D_MODEL)] + [_const_spec(a.shape) for a in (g, w_main, wqt, wvt, wf, wft, bfl, bft, cw)]
    return pl.pallas_call(
        functools.partial(_even_proj_kernel, tm=tm, tk=tk),
        grid=(b, nt), in_specs=in_specs, out_specs=out_specs, out_shape=out_shape,
        scratch_shapes=[pltpu.VMEM((tm + SUBLANES, A_W), F32),
                        pltpu.VMEM((SUBLANES, LANES), F32),
                        pltpu.VMEM((2 * SUBLANES, LANES), F32)],
        compiler_params=_params(("arbitrary", "arbitrary")),
        name="even_proj",
    )(x, g, w_main, wqt, wvt, wf, wft, bfl, bft, cw)


def _fox_kernel(qt_ref, k_ref, vt_ref, fc_ref, fr_ref, o_ref, qm_scr, m_scr, l_scr, acc_scr, *, tq):
    qi = pl.program_id(1)
    sub = lax.broadcasted_iota(jnp.int32, (LANES, tq), 0)
    for h in range(H_B):
        hp, hh = divmod(h, 2)
        qp = qt_ref[hp * LANES:(hp + 1) * LANES, :]
        in_head = (sub < HD_B) if hh == 0 else (sub >= HD_B)
        qm_scr[h] = jnp.where(in_head, qp, jnp.zeros_like(qp))
    m_scr[...] = jnp.full(m_scr.shape, NEG, F32)
    l_scr[...] = jnp.zeros_like(l_scr)
    acc_scr[...] = jnp.zeros_like(acc_scr)
    causal = (lax.broadcasted_iota(jnp.int32, (tq, tq), 0) <= lax.broadcasted_iota(jnp.int32, (tq, tq), 1))

    def step(j, masked):
        off = pl.multiple_of(j * tq, tq)
        group = lambda h: slice((h // 2) * LANES, (h // 2 + 1) * LANES)
        scores = [_dot(k_ref[pl.ds(off, tq), group(h)], qm_scr[h]) for h in range(H_B)]
        probs, alphas = [], []
        for h in range(H_B):
            s = scores[h] + (fr_ref[h:h + 1, :] - fc_ref[pl.ds(off, tq), h:h + 1])
            if masked:
                s = jnp.where(causal, s, NEG)
            m_old = m_scr[h:h + 1, :]
            m_new = jnp.maximum(m_old, jnp.max(s, axis=0, keepdims=True))
            p = jnp.exp(s - m_new)
            alpha = jnp.exp(m_old - m_new)
            l_scr[h:h + 1, :] = alpha * l_scr[h:h + 1, :] + jnp.sum(p, axis=0, keepdims=True)
            m_scr[h:h + 1, :] = m_new
            probs.append(p.astype(BF))
            alphas.append(alpha)
        for h in range(H_B):
            hp, hh = divmod(h, 2)
            rows = slice(hh * HD_B, (hh + 1) * HD_B)
            pv = _dot(vt_ref[j, group(h), :], probs[h])
            acc_scr[hp, rows, :] = alphas[h] * acc_scr[hp, rows, :] + pv[rows, :]

    def body(j, carry):
        step(j, False)
        return carry

    lax.fori_loop(0, qi, body, 0)
    step(qi, True)
    for hp in range(H_B // 2):
        denom = jnp.where(sub < HD_B, l_scr[2 * hp:2 * hp + 1, :], l_scr[2 * hp + 1:2 * hp + 2, :])
        o_ref[:, hp * LANES:(hp + 1) * LANES] = (acc_scr[hp] / denom).T.astype(BF)


def _fox_prompt(qt, kb, vt, fc, fr, *, tq):
    b, nq, _, _ = qt.shape
    s = nq * tq
    return pl.pallas_call(
        functools.partial(_fox_kernel, tq=tq),
        grid=(b, nq),
        in_specs=[
            pl.BlockSpec((None, None, B_W, tq), lambda i, t: (i, t, 0, 0)),
            pl.BlockSpec((None, s, B_W), lambda i, t: (i, 0, 0)),
            pl.BlockSpec((None, nq, B_W, tq), lambda i, t: (i, 0, 0, 0)),
            pl.BlockSpec((None, s, H_B), lambda i, t: (i, 0, 0)),
            pl.BlockSpec((None, None, H_B, tq), lambda i, t: (i, t, 0, 0)),
        ],
        out_specs=pl.BlockSpec((None, tq, B_W), lambda i, t: (i, t, 0)),
        out_shape=jax.ShapeDtypeStruct((b, s, B_W), BF),
        scratch_shapes=[pltpu.VMEM((H_B, LANES, tq), BF), pltpu.VMEM((H_B, tq), F32),
                        pltpu.VMEM((H_B, tq), F32), pltpu.VMEM((H_B // 2, LANES, tq), F32)],
        compiler_params=_params(("arbitrary", "arbitrary")),
        name="fox_prompt",
    )(qt, kb, vt, fc, fr)


def _mix_ffn_kernel(x_ref, ma_ref, mb_ref, wo_ref, gf_ref, wu_ref, cw_ref, cb_ref, wd_ref, gfin_ref,
                    y_ref, st_ref, acc_ref, h_ref, act_ref, ext_ref, carry_ref, *, tm, final_norm):
    t = pl.program_id(1)

    @pl.when(t == 0)
    def _():
        carry_ref[...] = jnp.zeros_like(carry_ref)

    half = wo_ref.shape[0] // 2
    x1 = x_ref[...] + _dot(ma_ref[...], wo_ref[0:half, :]) + _dot(mb_ref[...], wo_ref[half:2 * half, :])
    acc_ref[...] = x1
    h_ref[...] = _rms(x1, gf_ref[...]).astype(BF)

    def up(c):
        gate_cols = slice(c * FF_CHUNK, (c + 1) * FF_CHUNK)
        lin_cols = slice(D_FF + c * FF_CHUNK, D_FF + (c + 1) * FF_CHUNK)
        return _dot(h_ref[...], wu_ref[:, gate_cols]), _dot(h_ref[...], wu_ref[:, lin_cols])

    nxt = up(0)
    for c in range(N_FF):
        cols = slice(c * FF_CHUNK, (c + 1) * FF_CHUNK)
        g, u = nxt
        if c + 1 < N_FF:
            nxt = up(c + 1)
        ext = ext_ref.at[c % 2]
        ext[0:SUBLANES, :] = carry_ref[c]
        ext[SUBLANES:SUBLANES + tm, :] = g
        cw = cw_ref[:, cols]
        gconv = (cw[0:1] * ext[SUBLANES - 2:SUBLANES - 2 + tm, :]
                 + cw[1:2] * ext[SUBLANES - 1:SUBLANES - 1 + tm, :]
                 + cw[2:3] * g + cb_ref[:, cols])
        act_ref[:, cols] = (gconv * jax.nn.sigmoid(gconv) * u).astype(BF)
        carry_ref[c] = ext[tm:tm + SUBLANES, :]
        st_ref[:, cols] = ext[tm + SUBLANES - 2:tm + SUBLANES, :]
    y = acc_ref[...] + _dot(act_ref[...], wd_ref[...])
    y_ref[...] = _rms(y, gfin_ref[...]) if final_norm else y


def _mix_ffn(x, ma, mb, lane_blk_b, wo, gf, wu, cw, cb, wd, gfin, *, tm, final_norm):
    b, s, _ = x.shape
    nt = s // tm
    half = wo.shape[0] // 2
    return pl.pallas_call(
        functools.partial(_mix_ffn_kernel, tm=tm, final_norm=final_norm),
        grid=(b, nt),
        in_specs=[
            pl.BlockSpec((None, tm, D_MODEL), lambda i, t: (i, t, 0)),
            pl.BlockSpec((None, tm, half), lambda i, t: (i, t, 0)),
            pl.BlockSpec((None, tm, half), lambda i, t: (i, t, lane_blk_b)),
            _const_spec(wo.shape), _const_spec(gf.shape), _const_spec(wu.shape),
            _const_spec(cw.shape), _const_spec(cb.shape), _const_spec(wd.shape), _const_spec(gfin.shape),
        ],
        out_specs=(pl.BlockSpec((None, tm, D_MODEL), lambda i, t: (i, t, 0)),
                   pl.BlockSpec((None, 2, D_FF), lambda i, t: (i, 0, 0))),
        out_shape=(jax.ShapeDtypeStruct((b, s, D_MODEL), F32),
                   jax.ShapeDtypeStruct((b, 2, D_FF), F32)),
        scratch_shapes=[pltpu.VMEM((tm, D_MODEL), F32), pltpu.VMEM((tm, D_MODEL), BF),
                        pltpu.VMEM((tm, D_FF), BF),
                        pltpu.VMEM((2, tm + SUBLANES, FF_CHUNK), F32),
                        pltpu.VMEM((N_FF, SUBLANES, FF_CHUNK), F32)],
        compiler_params=_params(("arbitrary", "arbitrary")),
        name="mix_ffn",
    )(x, ma, mb, wo, gf, wu, cw, cb, wd, gfin)


def _mix_ffn_sample_kernel(x_ref, ma_ref, mb_ref, wo_ref, gf_ref, wu_ref, cw_ref, cb_ref, wd_ref,
                           gfin_ref, st_ref, y_ref, stn_ref, *, final_norm):
    half = wo_ref.shape[0] // 2
    x1 = x_ref[...] + _dot(ma_ref[...], wo_ref[0:half, :]) + _dot(mb_ref[...], wo_ref[half:2 * half, :])
    h = _rms(x1, gf_ref[...]).astype(BF)
    y_ref[...] = x1
    for c in range(N_FF):
        cols = slice(c * FF_CHUNK, (c + 1) * FF_CHUNK)
        cols1 = slice(D_FF + c * FF_CHUNK, D_FF + (c + 1) * FF_CHUNK)
        g = _dot(h, wu_ref[:, cols])
        u = _dot(h, wu_ref[:, cols1])
        cw = cw_ref[:, cols]
        prev1 = st_ref[:, cols1]
        gconv = cw[0:1] * st_ref[:, cols] + cw[1:2] * prev1 + cw[2:3] * g + cb_ref[:, cols]
        act = (gconv * jax.nn.sigmoid(gconv) * u).astype(BF)
        y_ref[...] += _dot(act, wd_ref[cols, :])
        stn_ref[:, cols] = prev1
        stn_ref[:, cols1] = g
    if final_norm:
        y_ref[...] = _rms(y_ref[...], gfin_ref[...])


def _mix_ffn_sample(x, ma, mb, wo, gf, wu, cw, cb, wd, gfin, st, *, final_norm):
    n = x.shape[0]
    return pl.pallas_call(
        functools.partial(_mix_ffn_sample_kernel, final_norm=final_norm),
        out_shape=(jax.ShapeDtypeStruct((n, D_MODEL), F32), jax.ShapeDtypeStruct((n, 2 * D_FF), F32)),
        compiler_params=pltpu.CompilerParams(vmem_limit_bytes=VMEM_LIMIT),
        name="mix_ffn_sample",
    )(x, ma, mb, wo, gf, wu, cw, cb, wd, gfin, st)


def _odd_kernel(x_ref, g_ref, w_ref, wg_ref, wgt_ref, bg_ref, bgt_ref, wp_ref, ps_ref,
                mix_ref, c_out, n_out, m_out, pool_out,
                c_scr, n_scr, m_scr, e_ref, s_a, s_b, hc_scr, *, tm):
    t = pl.program_id(1)
    top = 3 * SUBLANES

    @pl.when(t == 0)
    def _():
        c_scr[...] = jnp.zeros_like(c_scr)
        n_scr[...] = jnp.zeros_like(n_scr)
        m_scr[...] = jnp.zeros_like(m_scr)
        e_ref[0:top, :] = jnp.zeros((top, P_W), F32)
        s_a[0:SUBLANES, :] = jnp.zeros((SUBLANES, P_W), F32)
        s_b[0:SUBLANES, :] = jnp.zeros((SUBLANES, P_W), F32)

    h = _rms(x_ref[...], g_ref[...]).astype(BF)
    q = _dot(h, w_ref[:, 0:C_W]).astype(BF)
    kf = _dot(h, w_ref[:, C_W:2 * C_W]) * (DK_C ** -0.5)
    kb = kf.astype(BF)
    vb = _dot(h, w_ref[:, 2 * C_W:3 * C_W]).astype(BF)
    og = jax.nn.sigmoid(_dot(h, w_ref[:, 3 * C_W:4 * C_W]))
    p = _dot(h, w_ref[:, 4 * C_W:4 * C_W + P_W])

    gz = _dot(h, wg_ref[...]) + bg_ref[...]
    lf_c = _log_sigmoid(gz)
    gzt = _dot_nt(wgt_ref[...], h) + bgt_ref[...]
    lf_r = _log_sigmoid(gzt)

    tri_l = _tri(CHUNK, True)
    tri_u = _tri(CHUNK, False)
    r_i = lax.broadcasted_iota(jnp.int32, (CHUNK, CHUNK), 0)
    c_i = lax.broadcasted_iota(jnp.int32, (CHUNK, CHUNK), 1)
    causal = c_i <= r_i

    for ci in range(tm // CHUNK):
        rows = slice(ci * CHUNK, (ci + 1) * CHUNK)
        bcum_c = _tri_dot_left(tri_l, lf_c[rows, :])
        bcum_r = _tri_dot_right(lf_r[:, rows], tri_u)
        for hd in range(H_C):
            lanes = slice(hd * DK_C, (hd + 1) * DK_C)
            bc = bcum_c[:, H_C + hd:H_C + hd + 1]
            br = bcum_r[H_C + hd:H_C + hd + 1, :]
            li_c = gz[rows, hd:hd + 1]
            li_r = gzt[hd:hd + 1, rows]
            m_prev = m_scr[hd:hd + 1, 0:1]
            dlog = jnp.where(causal, bc + (li_r - br), NEG)
            inter = bc + m_prev
            m_t = jnp.maximum(inter, jnp.max(dlog, axis=-1, keepdims=True))
            qh = q[rows, lanes]
            kh = kb[rows, lanes]
            vh = vb[rows, lanes]
            sc = _dot_nt(qh, kh) * jnp.exp(dlog - m_t)
            w_inter = jnp.exp(inter - m_t)
            c_old = c_scr[hd]
            n_old = n_scr[hd:hd + 1, :]
            num = _dot(sc.astype(BF), vh) + w_inter * _dot(qh, c_old.astype(BF))
            qn = jnp.sum(qh.astype(F32) * _round_bf(n_old), axis=-1, keepdims=True)
            den = jnp.sum(sc, axis=-1, keepdims=True) + w_inter * qn
            hc_scr[rows, lanes] = num / jnp.maximum(jnp.abs(den), jnp.exp(-m_t))
            bl = bc[CHUNK - 1:CHUNK, :]
            g_c = bl - bc + li_c
            m_new = jnp.maximum(bl + m_prev, jnp.max(g_c, axis=0, keepdims=True))
            a = jnp.exp(bl + m_prev - m_new)
            wg = jnp.exp(g_c - m_new)
            wgk = wg * kf[rows, lanes]
            c_scr[hd] = a * c_old + _dot(wgk.T.astype(BF), vh)
            n_scr[hd:hd + 1, :] = a * n_old + jnp.sum(_round_bf(wg) * kh.astype(F32), axis=0, keepdims=True)
            m_scr[hd:hd + 1, :] = jnp.broadcast_to(m_new, (1, LANES))

    mix_ref[:, 0:C_W] = (og * hc_scr[...]).astype(BF)

    n_rows = tm + top
    e_ref[top:n_rows, :] = p
    lo = SUBLANES
    s_a[lo:n_rows, :] = e_ref[lo:n_rows, :] + e_ref[lo - 1:n_rows - 1, :]
    s_b[lo:n_rows, :] = s_a[lo:n_rows, :] + s_a[lo - 2:n_rows - 2, :]
    w2 = s_a[top:n_rows, 0:POOL_G]
    s_a[lo:n_rows, :] = s_b[lo:n_rows, :] + s_b[lo - 4:n_rows - 4, :]
    w4 = s_b[top:n_rows, POOL_G:2 * POOL_G]
    w8 = s_a[top:n_rows, 2 * POOL_G:3 * POOL_G]
    w16 = s_a[top:n_rows, 3 * POOL_G:4 * POOL_G] + s_a[top - 8:n_rows - 8, 3 * POOL_G:4 * POOL_G]
    pos1 = (t * tm + lax.broadcasted_iota(jnp.int32, (tm, 1), 0) + 1).astype(F32)
    for gi, (win, wsum) in enumerate(zip(POOL_WINDOWS, (w2, w4, w8, w16))):
        lanes = slice(gi * POOL_G, (gi + 1) * POOL_G)
        mean = wsum / jnp.minimum(float(win), pos1)
        y = _dot((mean - p[:, lanes]).astype(BF), wp_ref[gi]) * ps_ref[:, lanes]
        mix_ref[:, C_W + gi * POOL_G:C_W + (gi + 1) * POOL_G] = y.astype(BF)
    pool_out[...] = e_ref[n_rows - 2 * SUBLANES:n_rows, :]
    e_ref[SUBLANES:top, :] = e_ref[n_rows - 2 * SUBLANES:n_rows, :]

    c_out[...] = c_scr[...]
    n_out[...] = n_scr[...]
    m_out[...] = m_scr[...]


def _odd_prompt(x, g, w_main, wg, wgt, bg, bgt, wp, ps, *, tm):
    b, s, _ = x.shape
    nt = s // tm
    state = lambda *shape: pl.BlockSpec((None,) + shape, lambda i, t: (i,) + (0,) * len(shape))
    return pl.pallas_call(
        functools.partial(_odd_kernel, tm=tm),
        grid=(b, nt),
        in_specs=[pl.BlockSpec((None, tm, D_MODEL), lambda i, t: (i, t, 0))]
                 + [_const_spec(a.shape) for a in (g, w_main, wg, wgt, bg, bgt, wp, ps)],
        out_specs=(pl.BlockSpec((None, tm, C_W + P_W), lambda i, t: (i, t, 0)),
                   state(H_C, DK_C, DK_C), state(SUBLANES, DK_C), state(SUBLANES, LANES),
                   state(2 * SUBLANES, P_W)),
        out_shape=(jax.ShapeDtypeStruct((b, s, C_W + P_W), BF),
                   jax.ShapeDtypeStruct((b, H_C, DK_C, DK_C), F32),
                   jax.ShapeDtypeStruct((b, SUBLANES, DK_C), F32),
                   jax.ShapeDtypeStruct((b, SUBLANES, LANES), F32),
                   jax.ShapeDtypeStruct((b, 2 * SUBLANES, P_W), F32)),
        scratch_shapes=[pltpu.VMEM((H_C, DK_C, DK_C), F32), pltpu.VMEM((SUBLANES, DK_C), F32),
                        pltpu.VMEM((SUBLANES, LANES), F32),
                        pltpu.VMEM((tm + 3 * SUBLANES, P_W), F32),
                        pltpu.VMEM((tm + 3 * SUBLANES, P_W), F32),
                        pltpu.VMEM((tm + 3 * SUBLANES, P_W), F32),
                        pltpu.VMEM((tm, C_W), F32)],
        compiler_params=_params(("arbitrary", "arbitrary")),
        name="odd_prompt",
    )(x, g, w_main, wg, wgt, bg, bgt, wp, ps)


def _even_sample_kernel(x_ref, g_ref, w_ref, wf_ref, bf_ref, cw_ref, st_ref, seg_ref,
                        a_ref, q_ref, k_ref, v_ref, lf_ref, snew_ref, cn_ref):
    h = _rms(x_ref[...], g_ref[...]).astype(BF)
    u = _dot(h, w_ref[:, 0:A_W])
    gb = _dot(h, w_ref[:, A_W:2 * A_W])
    gc = _dot(h, w_ref[:, 2 * A_W:3 * A_W])
    cu = gc * u
    cw = cw_ref[...]
    prev1 = st_ref[:, A_W:2 * A_W]
    a_ref[...] = (gb * (cw[0:1] * st_ref[:, 0:A_W] + cw[1:2] * prev1 + cw[2:3] * cu)).astype(BF)
    cn_ref[:, 0:A_W] = prev1
    cn_ref[:, A_W:2 * A_W] = cu
    base = 3 * A_W
    qs = _round_bf(_dot(h, w_ref[:, base:base + B_W]) * (HD_B ** -0.5))
    k = _dot(h, w_ref[:, base + B_W:base + 2 * B_W])
    v = _dot(h, w_ref[:, base + 2 * B_W:base + 3 * B_W])
    k_ref[...] = k
    v_ref[...] = v
    lf_ref[...] = _log_sigmoid(_dot(h, wf_ref[...]) + bf_ref[...])[:, 0:H_B]
    snew_ref[...] = _tri_dot_right(qs * _round_bf(k), seg_ref[...])
    q_ref[...] = qs


def _even_sample(x, g, w_main, wf, bfl, cw, st, seg):
    n = x.shape[0]
    return pl.pallas_call(
        _even_sample_kernel,
        out_shape=(jax.ShapeDtypeStruct((n, A_W), BF),
                   jax.ShapeDtypeStruct((n, B_W), F32),
                   jax.ShapeDtypeStruct((n, B_W), F32),
                   jax.ShapeDtypeStruct((n, B_W), F32),
                   jax.ShapeDtypeStruct((n, H_B), F32),
                   jax.ShapeDtypeStruct((n, LANES), F32),
                   jax.ShapeDtypeStruct((n, 2 * A_W), F32)),
        compiler_params=pltpu.CompilerParams(vmem_limit_bytes=VMEM_LIMIT),
        name="even_sample",
    )(x, g, w_main, wf, bfl, cw, st, seg)


def _paged_kernel(pt_ref, qbd_ref, lfn_ref, snew_ref, vnew_ref, *refs, n_steps):
    npg = PAGES_PER_STEP
    k_refs = refs[0:npg]
    v_refs = refs[npg:2 * npg]
    f_refs = refs[2 * npg:3 * npg]
    o_ref = refs[3 * npg]
    m_scr, l_scr, acc_scr, suf_scr = refs[3 * npg + 1:]
    c = pl.program_id(1)
    own = (lax.broadcasted_iota(jnp.int32, (H_B, B_W), 1) // HD_B
           == lax.broadcasted_iota(jnp.int32, (H_B, B_W), 0))

    @pl.when(c == 0)
    def _():
        m_scr[...] = jnp.broadcast_to(snew_ref[...], m_scr.shape)
        l_scr[...] = jnp.ones_like(l_scr)
        acc_scr[...] = jnp.where(own, jnp.broadcast_to(_round_bf(vnew_ref[...]), (H_B, B_W)), 0.0)
        suf_scr[...] = jnp.broadcast_to(lfn_ref[...], suf_scr.shape)

    r_i = lax.broadcasted_iota(jnp.int32, (PAGE_SIZE, 2 * PAGE_SIZE), 0)
    c_i = lax.broadcasted_iota(jnp.int32, (PAGE_SIZE, 2 * PAGE_SIZE), 1)
    later_or_total = jnp.where((r_i > c_i) | (c_i >= PAGE_SIZE), 1.0, 0.0).astype(BF)
    lf_all = jnp.concatenate([f_refs[j][...] for j in range(npg)], axis=0)
    sums = _tri_dot_right(lf_all, later_or_total)

    q = qbd_ref[...]
    suf = suf_scr[:, 0:1]
    scores = []
    for j in reversed(range(npg)):
        rows = slice(j * H_B, (j + 1) * H_B)
        scores.append(_dot(q, k_refs[j][...].astype(BF)) + (sums[rows, 0:PAGE_SIZE] + suf))
        suf = suf + sums[rows, PAGE_SIZE:PAGE_SIZE + 1]
    suf_scr[...] = jnp.broadcast_to(suf, suf_scr.shape)
    m_old = m_scr[:, 0:1]
    m_new = m_old
    for s in scores:
        m_new = jnp.maximum(m_new, jnp.max(s, axis=-1, keepdims=True))
    alpha = jnp.exp(m_old - m_new)
    l = alpha * l_scr[:, 0:1]
    acc_scr[...] = alpha * acc_scr[...]
    for s, j in zip(scores, reversed(range(npg))):
        p = jnp.exp(s - m_new)
        l = l + jnp.sum(p, axis=-1, keepdims=True)
        acc_scr[...] += _dot_nt(p.astype(BF), v_refs[j][...].astype(BF))
    m_scr[...] = jnp.broadcast_to(m_new, m_scr.shape)
    l_scr[...] = jnp.broadcast_to(l, l_scr.shape)

    @pl.when(c == n_steps - 1)
    def _():
        o_ref[...] = jnp.sum(jnp.where(own, acc_scr[...] / l, 0.0), axis=0, keepdims=True)


def _paged_attention(page_table, qbd, lfn, snew, vnew, cache_k, cache_v, cache_lft):
    n, n_pages = page_table.shape
    npg = PAGES_PER_STEP
    n_steps = n_pages // npg

    def page_of(i, c, pt, j):
        return pt[i, (n_steps - 1 - c) * npg + j]

    per_sample = lambda *shape: pl.BlockSpec((None,) + shape, lambda i, c, pt: (i,) + (0,) * len(shape))
    kv_spec = lambda j: pl.BlockSpec((None, B_W, PAGE_SIZE), lambda i, c, pt: (page_of(i, c, pt, j), 0, 0))
    lf_spec = lambda j: pl.BlockSpec((None, H_B, PAGE_SIZE), lambda i, c, pt: (page_of(i, c, pt, j), 0, 0))
    in_specs = ([per_sample(H_B, B_W), per_sample(H_B, 1), per_sample(H_B, 1), per_sample(1, B_W)]
                + [kv_spec(j) for j in range(npg)] + [kv_spec(j) for j in range(npg)]
                + [lf_spec(j) for j in range(npg)])
    grid_spec = pltpu.PrefetchScalarGridSpec(
        num_scalar_prefetch=1, grid=(n, n_steps), in_specs=in_specs,
        out_specs=per_sample(1, B_W),
        scratch_shapes=[pltpu.VMEM((H_B, LANES), F32), pltpu.VMEM((H_B, LANES), F32),
                        pltpu.VMEM((H_B, B_W), F32), pltpu.VMEM((H_B, LANES), F32)])
    return pl.pallas_call(
        functools.partial(_paged_kernel, n_steps=n_steps),
        grid_spec=grid_spec,
        out_shape=jax.ShapeDtypeStruct((n, 1, B_W), F32),
        compiler_params=_params(("arbitrary", "arbitrary")),
        name="paged_attention",
    )(page_table, qbd, lfn, snew, vnew, *([cache_k] * npg), *([cache_v] * npg), *([cache_lft] * npg))


def _odd_sample_kernel(x_ref, g_ref, w_ref, wg_ref, bg_ref, wp_ref, ps_ref, c_ref, n_ref, m_ref, pool_ref,
                       mix_ref, c_out, n_out, m_out, pool_out, *, pool_div):
    nb = x_ref.shape[0]
    h = _rms(x_ref[...], g_ref[...]).astype(BF)
    q = _round_bf(_dot(h, w_ref[:, 0:C_W]))
    kf = _dot(h, w_ref[:, C_W:2 * C_W]) * (DK_C ** -0.5)
    k = _round_bf(kf)
    v = _round_bf(_dot(h, w_ref[:, 2 * C_W:3 * C_W]))
    og = jax.nn.sigmoid(_dot(h, w_ref[:, 3 * C_W:4 * C_W]))
    p = _dot(h, w_ref[:, 4 * C_W:4 * C_W + P_W])
    gz = _dot(h, wg_ref[...]) + bg_ref[...]
    lf = _log_sigmoid(gz)
    eye = (lax.broadcasted_iota(jnp.int32, (DK_C, DK_C), 0)
           == lax.broadcasted_iota(jnp.int32, (DK_C, DK_C), 1))

    for hd in range(H_C):
        lanes = slice(hd * DK_C, (hd + 1) * DK_C)
        li = gz[:, hd:hd + 1]
        lfh = lf[:, H_C + hd:H_C + hd + 1]
        m_prev = m_ref[:, hd:hd + 1]
        qh = q[:, lanes]
        kh = k[:, lanes]
        vh = v[:, lanes]
        n_old = n_ref[:, lanes]
        m_t = jnp.maximum(lfh + m_prev, li)
        sc = jnp.sum(qh * kh, axis=-1, keepdims=True) * jnp.exp(li - m_t)
        w_inter = jnp.exp(lfh + m_prev - m_t)
        qc = jnp.concatenate(
            [_dot(qh[i:i + 1, :].astype(BF), c_ref[i, hd].astype(BF)) for i in range(nb)], axis=0)
        num = _round_bf(sc) * vh + w_inter * qc
        den = sc + w_inter * jnp.sum(qh * _round_bf(n_old), axis=-1, keepdims=True)
        hout = num / jnp.maximum(jnp.abs(den), jnp.exp(-m_t))
        mix_ref[:, lanes] = (og[:, lanes] * hout).astype(BF)
        a = w_inter
        wg = jnp.exp(li - m_t)
        wgk = wg * kf[:, lanes]
        n_out[:, lanes] = a * n_old + _round_bf(wg) * kh
        m_out[:, hd:hd + 1] = m_t
        for i in range(nb):
            kd = jnp.where(eye, jnp.broadcast_to(wgk[i:i + 1, :], (DK_C, DK_C)), 0.0).astype(BF)
            vrep = jnp.broadcast_to(vh[i:i + 1, :], (DK_C, DK_C)).astype(BF)
            c_out[i, hd] = a[i:i + 1, :] * c_ref[i, hd] + _dot(kd, vrep)

    prev = pool_ref[...]
    row = lax.broadcasted_iota(jnp.int32, prev.shape, 1)
    for gi, win in enumerate(POOL_WINDOWS):
        lanes = slice(gi * POOL_G, (gi + 1) * POOL_G)
        tail = jnp.sum(jnp.where(row >= POOL_PREV - (win - 1), prev, 0.0)[:, :, lanes], axis=1)
        mean = (tail + p[:, lanes]) / pool_div[gi]
        y = _dot((mean - p[:, lanes]).astype(BF), wp_ref[gi]) * ps_ref[:, lanes]
        mix_ref[:, C_W + gi * POOL_G:C_W + (gi + 1) * POOL_G] = y.astype(BF)
    pool_out[:, 0:POOL_PREV - 1, :] = pool_ref[:, 1:POOL_PREV, :]
    for i in range(nb):
        pool_out[i, POOL_PREV - 1:POOL_PREV, :] = p[i:i + 1, :]


def _odd_sample(x, g, w_main, wg, bg, wp, ps, c, n, m, pool, *, pool_div, nb):
    nsmp = x.shape[0]
    rows = lambda w: pl.BlockSpec((nb, w), lambda i: (i, 0))
    return pl.pallas_call(
        functools.partial(_odd_sample_kernel, pool_div=pool_div),
        grid=(nsmp // nb,),
        in_specs=[rows(D_MODEL)] + [_const_spec(a.shape) for a in (g, w_main, wg, bg, wp, ps)]
                 + [pl.BlockSpec((nb, H_C, DK_C, DK_C), lambda i: (i, 0, 0, 0)),
                    rows(C_W), rows(H_C),
                    pl.BlockSpec((nb, POOL_PREV, P_W), lambda i: (i, 0, 0))],
        out_specs=(rows(C_W + P_W),
                   pl.BlockSpec((nb, H_C, DK_C, DK_C), lambda i: (i, 0, 0, 0)),
                   rows(C_W), rows(H_C),
                   pl.BlockSpec((nb, POOL_PREV, P_W), lambda i: (i, 0, 0))),
        out_shape=(jax.ShapeDtypeStruct((nsmp, C_W + P_W), BF),
                   jax.ShapeDtypeStruct((nsmp, H_C, DK_C, DK_C), F32),
                   jax.ShapeDtypeStruct((nsmp, C_W), F32),
                   jax.ShapeDtypeStruct((nsmp, H_C), F32),
                   jax.ShapeDtypeStruct((nsmp, POOL_PREV, P_W), F32)),
        compiler_params=_params(("arbitrary",)),
        name="odd_sample",
    )(x, g, w_main, wg, bg, wp, ps, c, n, m, pool)


def _pad_cols(w, n):
    return jnp.pad(w, ((0, 0), (0, n - w.shape[1])))


def _ffn_weights(w_up, conv_w, conv_b, w_down):
    return w_up.astype(BF), conv_w, conv_b.reshape(1, D_FF), w_down.astype(BF)


def kernel(x_prompt, x_sample, cache_k, cache_v, cache_logf, state_conv_a, state_mlstm_c, state_mlstm_n,
           state_mlstm_m, state_pool, state_ffn_conv, page_table, norm_mix, norm_ffn, norm_final, w_in_even,
           b_forget_even, conv_a, w_out_even, w_in_odd, b_igate_odd, b_fgate_odd, w_pool_odd, pool_scale_odd,
           w_out_odd, w_up, ffn_conv_w, ffn_conv_b, w_down):
    b, s, _ = x_prompt.shape
    nsmp = x_sample.shape[0]
    n_pages = page_table.shape[1]
    assert x_sample.shape[1] == 1 and norm_mix.shape[0] == 2
    tm = min(512, s)
    tq = min(256, s)
    assert s % tm == 0 and n_pages % PAGES_PER_STEP == 0

    xs = x_sample.reshape(nsmp, D_MODEL)
    row = lambda v: v.reshape(1, -1)
    g_final = row(norm_final)

    w = w_in_even[0]
    n_main = 3 * A_W + 3 * B_W
    w_main = w[:, :n_main].astype(BF)
    wf = _pad_cols(w[:, n_main:], LANES).astype(BF)
    wft = jnp.pad(w[:, n_main:].T, ((0, 2 * SUBLANES - H_B), (0, 0))).astype(BF)
    bfl = _pad_cols(row(b_forget_even[0]), LANES)
    bft = jnp.pad(b_forget_even[0].reshape(H_B, 1), ((0, 2 * SUBLANES - H_B), (0, 0)))
    g_mix = row(norm_mix[0])
    wo = w_out_even[0].astype(BF)
    ffn0 = _ffn_weights(w_up[0], ffn_conv_w[0], ffn_conv_b[0], w_down[0])

    wqt = w[:, 3 * A_W:3 * A_W + B_W].T.astype(BF)
    wvt = w[:, 3 * A_W + 2 * B_W:n_main].T.astype(BF)
    a_p, qt_p, k_p, v_p, kb_p, vt_p, lf_p, fc_p, fr_p, ca_p = _even_proj(
        x_prompt, g_mix, w_main, wqt, wvt, wf, wft, bfl, bft, conv_a[0], tm=tm, tk=tq)
    att_p = _fox_prompt(qt_p, kb_p, vt_p, fc_p, fr_p, tq=tq)
    xp, ff0_p = _mix_ffn(x_prompt, a_p, att_p, 0, wo, row(norm_ffn[0]), *ffn0, g_final, tm=tm, final_norm=False)

    seg = jnp.asarray((np.arange(B_W)[:, None] // HD_B == np.arange(LANES)[None, :]).astype(np.float32), dtype=BF)
    a_s, q_s, k_s, v_s, lf_s, snew_s, ca_s = _even_sample(
        xs, g_mix, w_main, wf, bfl, conv_a[0], state_conv_a[0].reshape(nsmp, 2 * A_W), seg)
    head_cols = jnp.asarray(np.arange(B_W)[None, :] // HD_B == np.arange(H_B)[:, None])
    qbd = jnp.where(head_cols[None], q_s[:, None, :], 0.0).astype(BF)
    n_phys = cache_k.shape[1]
    kt = cache_k[0].transpose(0, 2, 3, 1).reshape(n_phys, B_W, PAGE_SIZE)
    vt = cache_v[0].transpose(0, 2, 3, 1).reshape(n_phys, B_W, PAGE_SIZE)
    att_s = _paged_attention(
        page_table, qbd, lf_s.reshape(nsmp, H_B, 1), snew_s[:, :H_B].reshape(nsmp, H_B, 1),
        v_s.reshape(nsmp, 1, B_W), kt, vt, cache_logf[0].transpose(0, 2, 1))
    xs1, ff0_s = _mix_ffn_sample(xs, a_s, att_s.reshape(nsmp, B_W).astype(BF), wo, row(norm_ffn[0]), *ffn0,
                                 g_final, state_ffn_conv[0].reshape(nsmp, 2 * D_FF), final_norm=False)

    w = w_in_odd[0]
    n_qkvo = 4 * C_W
    w_main = jnp.concatenate([w[:, :n_qkvo], w[:, n_qkvo + 2 * H_C:]], axis=1).astype(BF)
    w_gate = w[:, n_qkvo:n_qkvo + 2 * H_C]
    wg = _pad_cols(w_gate, LANES).astype(BF)
    wgt = jnp.pad(w_gate.T, ((0, 2 * SUBLANES - 2 * H_C), (0, 0))).astype(BF)
    b_gate = jnp.concatenate([b_igate_odd[0], b_fgate_odd[0]])
    bg = _pad_cols(row(b_gate), LANES)
    bgt = jnp.pad(b_gate.reshape(2 * H_C, 1), ((0, 2 * SUBLANES - 2 * H_C), (0, 0)))
    g_mix = row(norm_mix[1])
    wp = w_pool_odd[0].astype(BF)
    ps = row(pool_scale_odd[0])
    wo = w_out_odd[0].astype(BF)
    ffn1 = _ffn_weights(w_up[1], ffn_conv_w[1], ffn_conv_b[1], w_down[1])

    mix_p, c_p, n_p, m_p, pool_p = _odd_prompt(xp, g_mix, w_main, wg, wgt, bg, bgt, wp, ps, tm=tm)
    y_p, ff1_p = _mix_ffn(xp, mix_p, mix_p, 1, wo, row(norm_ffn[1]), *ffn1, g_final, tm=tm, final_norm=True)

    pos0 = n_pages * PAGE_SIZE
    pool_div = tuple(float(min(win, pos0 + 1)) for win in POOL_WINDOWS)
    mix_s, c_s, n_s, m_s, pool_s = _odd_sample(
        xs1, g_mix, w_main, wg, bg, wp, ps, state_mlstm_c[0], state_mlstm_n[0].reshape(nsmp, C_W),
        state_mlstm_m[0], state_pool[0], pool_div=pool_div, nb=min(8, nsmp))
    y_s, ff1_s = _mix_ffn_sample(xs1, mix_s[:, :C_W], mix_s[:, C_W:], wo, row(norm_ffn[1]), *ffn1, g_final,
                                 state_ffn_conv[1].reshape(nsmp, 2 * D_FF), final_norm=True)

    heads = lambda z: z.reshape(z.shape[:-1] + (H_B, HD_B))
    return (
        y_p, y_s.reshape(nsmp, 1, D_MODEL),
        heads(k_p)[None], heads(k_s).reshape(1, nsmp, 1, H_B, HD_B),
        heads(v_p)[None], heads(v_s).reshape(1, nsmp, 1, H_B, HD_B),
        lf_p[None], lf_s.reshape(1, nsmp, 1, H_B),
        ca_p[None], ca_s.reshape(1, nsmp, 2, A_W),
        c_p[None], c_s[None],
        n_p[:, :H_C][None], n_s.reshape(1, nsmp, H_C, DK_C),
        m_p[:, :H_C, 0][None], m_s[None],
        pool_p[:, 1:][None], pool_s[None],
        jnp.stack([ff0_p, ff1_p]),
        jnp.stack([ff0_s.reshape(nsmp, 2, D_FF), ff1_s.reshape(nsmp, 2, D_FF)]),
    )
```

```python
import functools

import numpy as np
import jax
import jax.numpy as jnp
from jax import lax
from jax.experimental import pallas as pl
from jax.experimental.pallas import tpu as pltpu

D_MODEL = 1024
A_W = 512
B_W = 512
H_B = 8
HD_B = 64
C_W = 512
H_C = 4
DK_C = 128
P_W = 512
POOL_WINDOWS = (2, 4, 8, 16)
POOL_G = 128
POOL_PREV = 15
D_FF = 2816
PAGE_SIZE = 128
CHUNK = 128
EPS = 1e-6

LANES = 128
SUBLANES = 8
FF_CHUNK = 256
N_FF = D_FF // FF_CHUNK
PAGES_PER_STEP = 16
VMEM_LIMIT = 56 * 1024 * 1024
NEG = -1e30

BF = jnp.bfloat16
F32 = jnp.float32


def _rms(x, g):
    return x * lax.rsqrt(jnp.mean(x * x, axis=-1, keepdims=True) + EPS) * g


def _dot(a, b):
    return jnp.dot(a, b, preferred_element_type=F32)


def _dot_nt(a, b):
    return lax.dot_general(a, b, (((1,), (1,)), ((), ())), preferred_element_type=F32)


def _split3(x):
    hi = x.astype(BF)
    r = x - hi.astype(F32)
    mid = r.astype(BF)
    lo = (r - mid.astype(F32)).astype(BF)
    return hi, mid, lo


def _tri_dot_left(tri, x):
    hi, mid, lo = _split3(x)
    return _dot(tri, hi) + _dot(tri, mid) + _dot(tri, lo)


def _tri_dot_right(x, tri):
    hi, mid, lo = _split3(x)
    return _dot(hi, tri) + _dot(mid, tri) + _dot(lo, tri)


def _log_sigmoid(x):
    return jnp.minimum(x, 0.0) - jnp.log1p(jnp.exp(-jnp.abs(x)))


def _tri(n, lower):
    r = lax.broadcasted_iota(jnp.int32, (n, n), 0)
    c = lax.broadcasted_iota(jnp.int32, (n, n), 1)
    keep = (c <= r) if lower else (r <= c)
    return jnp.where(keep, 1.0, 0.0).astype(BF)


def _round_bf(x):
    return x.astype(BF).astype(F32)


def _const_spec(shape):
    zeros = (0,) * len(shape)
    return pl.BlockSpec(shape, lambda *_: zeros, pipeline_mode=pl.Buffered(1))


def _params(sem):
    return pltpu.CompilerParams(dimension_semantics=sem, vmem_limit_bytes=VMEM_LIMIT)


def _even_proj_kernel(x_ref, g_ref, w_ref, wqt_ref, wvt_ref, wf_ref, wft_ref, bf_ref, bft_ref, cw_ref,
                      a_ref, qt_ref, k_ref, v_ref, kb_ref, vt_ref, lf_ref, fc_ref, fr_ref, cn_ref,
                      ext_ref, ccol_ref, crow_ref, *, tm, tk):
    t = pl.program_id(1)

    @pl.when(t == 0)
    def _():
        ext_ref[0:SUBLANES, :] = jnp.zeros((SUBLANES, A_W), F32)
        ccol_ref[...] = jnp.zeros_like(ccol_ref)
        crow_ref[...] = jnp.zeros_like(crow_ref)

    h = _rms(x_ref[...], g_ref[...]).astype(BF)
    lf = _log_sigmoid(_dot(h, wf_ref[...]) + bf_ref[...])
    lft = _log_sigmoid(_dot_nt(wft_ref[...], h) + bft_ref[...])
    lf_ref[...] = lf[:, 0:H_B]

    u = _dot(h, w_ref[:, 0:A_W])
    gb = _dot(h, w_ref[:, A_W:2 * A_W])
    gc = _dot(h, w_ref[:, 2 * A_W:3 * A_W])

    cs = _tri_dot_left(_tri(tm, True), lf) + ccol_ref[0:1, :]
    fc_ref[...] = cs[:, 0:H_B]
    ccol_ref[...] = jnp.broadcast_to(cs[tm - 1:tm, :], ccol_ref.shape)
    cst = _tri_dot_right(lft, _tri(tm, False)) + crow_ref[:, 0:1]
    for j in range(tm // tk):
        fr_ref[j] = cst[0:H_B, j * tk:(j + 1) * tk]
    crow_ref[...] = jnp.broadcast_to(cst[:, tm - 1:tm], crow_ref.shape)

    cu = gc * u
    ext_ref[SUBLANES:SUBLANES + tm, :] = cu
    cw = cw_ref[...]
    conv = (cw[0:1] * ext_ref[SUBLANES - 2:SUBLANES - 2 + tm, :]
            + cw[1:2] * ext_ref[SUBLANES - 1:SUBLANES - 1 + tm, :]
            + cw[2:3] * cu)
    a_ref[...] = (gb * conv).astype(BF)
    cn_ref[...] = ext_ref[tm + SUBLANES - 2:tm + SUBLANES, :]
    ext_ref[0:SUBLANES, :] = ext_ref[tm:tm + SUBLANES, :]

    base = 3 * A_W
    k = _dot(h, w_ref[:, base + B_W:base + 2 * B_W])
    k_ref[...] = k
    kb_ref[...] = k.astype(BF)
    v_ref[...] = _dot(h, w_ref[:, base + 2 * B_W:base + 3 * B_W])
    qt = (_dot_nt(wqt_ref[...], h) * (HD_B ** -0.5)).astype(BF)
    vt = _dot_nt(wvt_ref[...], h).astype(BF)
    for j in range(tm // tk):
        qt_ref[j] = qt[:, j * tk:(j + 1) * tk]
        vt_ref[j] = vt[:, j * tk:(j + 1) * tk]


def _even_proj(x, g, w_main, wqt, wvt, wf, wft, bfl, bft, cw, *, tm, tk):
    b, s, _ = x.shape
    nt = s // tm
    row = lambda w: pl.BlockSpec((None, tm, w), lambda i, t: (i, t, 0))
    chunked = lambda rows: pl.BlockSpec((None, tm // tk, rows, tk), lambda i, t: (i, t, 0, 0))
    out_shape = (
        jax.ShapeDtypeStruct((b, s, A_W), BF),
        jax.ShapeDtypeStruct((b, s // tk, B_W, tk), BF),
        jax.ShapeDtypeStruct((b, s, B_W), F32),
        jax.ShapeDtypeStruct((b, s, B_W), F32),
        jax.ShapeDtypeStruct((b, s, B_W), BF),
        jax.ShapeDtypeStruct((b, s // tk, B_W, tk), BF),
        jax.ShapeDtypeStruct((b, s, H_B), F32),
        jax.ShapeDtypeStruct((b, s, H_B), F32),
        jax.ShapeDtypeStruct((b, s // tk, H_B, tk), F32),
        jax.ShapeDtypeStruct((b, 2, A_W), F32),
    )
    out_specs = (
        row(A_W), chunked(B_W), row(B_W), row(B_W), row(B_W), chunked(B_W), row(H_B), row(H_B),
        chunked(H_B),
        pl.BlockSpec((None, 2, A_W), lambda i, t: (i, 0, 0)),
    )
    in_specs = [row(D_MODEL)] + [_const_spec(a.shape) for a in (g, w_main, wqt, wvt, wf, wft, bfl, bft, cw)]
    return pl.pallas_call(
        functools.partial(_even_proj_kernel, tm=tm, tk=tk),
        grid=(b, nt), in_specs=in_specs, out_specs=out_specs, out_shape=out_shape,
        scratch_shapes=[pltpu.VMEM((tm + SUBLANES, A_W), F32),
                        pltpu.VMEM((SUBLANES, LANES), F32),
                        pltpu.VMEM((2 * SUBLANES, LANES), F32)],
        compiler_params=_params(("arbitrary", "arbitrary")),
        name="even_proj",
    )(x, g, w_main, wqt, wvt, wf, wft, bfl, bft, cw)


def _fox_kernel(qt_ref, k_ref, vt_ref, fc_ref, fr_ref, o_ref, qm_scr, m_scr, l_scr, acc_scr, *, tq):
    qi = pl.program_id(1)
    sub = lax.broadcasted_iota(jnp.int32, (LANES, tq), 0)
    for h in range(H_B):
        hp, hh = divmod(h, 2)
        qp = qt_ref[hp * LANES:(hp + 1) * LANES, :]
        in_head = (sub < HD_B) if hh == 0 else (sub >= HD_B)
        qm_scr[h] = jnp.where(in_head, qp, jnp.zeros_like(qp))
    m_scr[...] = jnp.full(m_scr.shape, NEG, F32)
    l_scr[...] = jnp.zeros_like(l_scr)
    acc_scr[...] = jnp.zeros_like(acc_scr)
    causal = (lax.broadcasted_iota(jnp.int32, (tq, tq), 0) <= lax.broadcasted_iota(jnp.int32, (tq, tq), 1))

    def step(j, masked):
        off = pl.multiple_of(j * tq, tq)
        group = lambda h: slice((h // 2) * LANES, (h // 2 + 1) * LANES)
        scores = [_dot(k_ref[pl.ds(off, tq), group(h)], qm_scr[h]) for h in range(H_B)]
        probs, alphas = [], []
        for h in range(H_B):
            s = scores[h] + (fr_ref[h:h + 1, :] - fc_ref[pl.ds(off, tq), h:h + 1])
            if masked:
                s = jnp.where(causal, s, NEG)
            m_old = m_scr[h:h + 1, :]
            m_new = jnp.maximum(m_old, jnp.max(s, axis=0, keepdims=True))
            p = jnp.exp(s - m_new)
            alpha = jnp.exp(m_old - m_new)
            l_scr[h:h + 1, :] = alpha * l_scr[h:h + 1, :] + jnp.sum(p, axis=0, keepdims=True)
            m_scr[h:h + 1, :] = m_new
            probs.append(p.astype(BF))
            alphas.append(alpha)
        for h in range(H_B):
            hp, hh = divmod(h, 2)
            rows = slice(hh * HD_B, (hh + 1) * HD_B)
            pv = _dot(vt_ref[j, group(h), :], probs[h])
            acc_scr[hp, rows, :] = alphas[h] * acc_scr[hp, rows, :] + pv[rows, :]

    def body(j, carry):
        step(j, False)
        return carry

    lax.fori_loop(0, qi, body, 0)
    step(qi, True)
    for hp in range(H_B // 2):
        denom = jnp.where(sub < HD_B, l_scr[2 * hp:2 * hp + 1, :], l_scr[2 * hp + 1:2 * hp + 2, :])
        o_ref[:, hp * LANES:(hp + 1) * LANES] = (acc_scr[hp] / denom).T.astype(BF)


def _fox_prompt(qt, kb, vt, fc, fr, *, tq):
    b, nq, _, _ = qt.shape
    s = nq * tq
    return pl.pallas_call(
        functools.partial(_fox_kernel, tq=tq),
        grid=(b, nq),
        in_specs=[
            pl.BlockSpec((None, None, B_W, tq), lambda i, t: (i, t, 0, 0)),
            pl.BlockSpec((None, s, B_W), lambda i, t: (i, 0, 0)),
            pl.BlockSpec((None, nq, B_W, tq), lambda i, t: (i, 0, 0, 0)),
            pl.BlockSpec((None, s, H_B), lambda i, t: (i, 0, 0)),
            pl.BlockSpec((None, None, H_B, tq), lambda i, t: (i, t, 0, 0)),
        ],
        out_specs=pl.BlockSpec((None, tq, B_W), lambda i, t: (i, t, 0)),
        out_shape=jax.ShapeDtypeStruct((b, s, B_W), BF),
        scratch_shapes=[pltpu.VMEM((H_B, LANES, tq), BF), pltpu.VMEM((H_B, tq), F32),
                        pltpu.VMEM((H_B, tq), F32), pltpu.VMEM((H_B // 2, LANES, tq), F32)],
        compiler_params=_params(("arbitrary", "arbitrary")),
        name="fox_prompt",
    )(qt, kb, vt, fc, fr)


def _mix_ffn_kernel(x_ref, ma_ref, mb_ref, wo_ref, gf_ref, wu_ref, cw_ref, cb_ref, wd_ref, gfin_ref,
                    y_ref, st_ref, acc_ref, h_ref, act_ref, ext_ref, carry_ref, *, tm, final_norm):
    t = pl.program_id(1)

    @pl.when(t == 0)
    def _():
        carry_ref[...] = jnp.zeros_like(carry_ref)

    half = wo_ref.shape[0] // 2
    x1 = x_ref[...] + _dot(ma_ref[...], wo_ref[0:half, :]) + _dot(mb_ref[...], wo_ref[half:2 * half, :])
    acc_ref[...] = x1
    h_ref[...] = _rms(x1, gf_ref[...]).astype(BF)

    def up(c):
        gate_cols = slice(c * FF_CHUNK, (c + 1) * FF_CHUNK)
        lin_cols = slice(D_FF + c * FF_CHUNK, D_FF + (c + 1) * FF_CHUNK)
        return _dot(h_ref[...], wu_ref[:, gate_cols]), _dot(h_ref[...], wu_ref[:, lin_cols])

    nxt = up(0)
    for c in range(N_FF):
        cols = slice(c * FF_CHUNK, (c + 1) * FF_CHUNK)
        g, u = nxt
        if c + 1 < N_FF:
            nxt = up(c + 1)
        ext = ext_ref.at[c % 2]
        ext[0:SUBLANES, :] = carry_ref[c]
        ext[SUBLANES:SUBLANES + tm, :] = g
        cw = cw_ref[:, cols]
        gconv = (cw[0:1] * ext[SUBLANES - 2:SUBLANES - 2 + tm, :]
                 + cw[1:2] * ext[SUBLANES - 1:SUBLANES - 1 + tm, :]
                 + cw[2:3] * g + cb_ref[:, cols])
        act_ref[:, cols] = (gconv * jax.nn.sigmoid(gconv) * u).astype(BF)
        carry_ref[c] = ext[tm:tm + SUBLANES, :]
        st_ref[:, cols] = ext[tm + SUBLANES - 2:tm + SUBLANES, :]
    y = acc_ref[...] + _dot(act_ref[...], wd_ref[...])
    y_ref[...] = _rms(y, gfin_ref[...]) if final_norm else y


def _mix_ffn(x, ma, mb, lane_blk_b, wo, gf, wu, cw, cb, wd, gfin, *, tm, final_norm):
    b, s, _ = x.shape
    nt = s // tm
    half = wo.shape[0] // 2
    return pl.pallas_call(
        functools.partial(_mix_ffn_kernel, tm=tm, final_norm=final_norm),
        grid=(b, nt),
        in_specs=[
            pl.BlockSpec((None, tm, D_MODEL), lambda i, t: (i, t, 0)),
            pl.BlockSpec((None, tm, half), lambda i, t: (i, t, 0)),
            pl.BlockSpec((None, tm, half), lambda i, t: (i, t, lane_blk_b)),
            _const_spec(wo.shape), _const_spec(gf.shape), _const_spec(wu.shape),
            _const_spec(cw.shape), _const_spec(cb.shape), _const_spec(wd.shape), _const_spec(gfin.shape),
        ],
        out_specs=(pl.BlockSpec((None, tm, D_MODEL), lambda i, t: (i, t, 0)),
                   pl.BlockSpec((None, 2, D_FF), lambda i, t: (i, 0, 0))),
        out_shape=(jax.ShapeDtypeStruct((b, s, D_MODEL), F32),
                   jax.ShapeDtypeStruct((b, 2, D_FF), F32)),
        scratch_shapes=[pltpu.VMEM((tm, D_MODEL), F32), pltpu.VMEM((tm, D_MODEL), BF),
                        pltpu.VMEM((tm, D_FF), BF),
                        pltpu.VMEM((2, tm + SUBLANES, FF_CHUNK), F32),
                        pltpu.VMEM((N_FF, SUBLANES, FF_CHUNK), F32)],
        compiler_params=_params(("arbitrary", "arbitrary")),
        name="mix_ffn",
    )(x, ma, mb, wo, gf, wu, cw, cb, wd, gfin)


def _mix_ffn_sample_kernel(x_ref, ma_ref, mb_ref, wo_ref, gf_ref, wu_ref, cw_ref, cb_ref, wd_ref,
                           gfin_ref, st_ref, y_ref, stn_ref, *, final_norm):
    half = wo_ref.shape[0] // 2
    x1 = x_ref[...] + _dot(ma_ref[...], wo_ref[0:half, :]) + _dot(mb_ref[...], wo_ref[half:2 * half, :])
    h = _rms(x1, gf_ref[...]).astype(BF)
    y_ref[...] = x1
    for c in range(N_FF):
        cols = slice(c * FF_CHUNK, (c + 1) * FF_CHUNK)
        cols1 = slice(D_FF + c * FF_CHUNK, D_FF + (c + 1) * FF_CHUNK)
        g = _dot(h, wu_ref[:, cols])
        u = _dot(h, wu_ref[:, cols1])
        cw = cw_ref[:, cols]
        prev1 = st_ref[:, cols1]
        gconv = cw[0:1] * st_ref[:, cols] + cw[1:2] * prev1 + cw[2:3] * g + cb_ref[:, cols]
        act = (gconv * jax.nn.sigmoid(gconv) * u).astype(BF)
        y_ref[...] += _dot(act, wd_ref[cols, :])
        stn_ref[:, cols] = prev1
        stn_ref[:, cols1] = g
    if final_norm:
        y_ref[...] = _rms(y_ref[...], gfin_ref[...])


def _mix_ffn_sample(x, ma, mb, wo, gf, wu, cw, cb, wd, gfin, st, *, final_norm):
    n = x.shape[0]
    return pl.pallas_call(
        functools.partial(_mix_ffn_sample_kernel, final_norm=final_norm),
        out_shape=(jax.ShapeDtypeStruct((n, D_MODEL), F32), jax.ShapeDtypeStruct((n, 2 * D_FF), F32)),
        compiler_params=pltpu.CompilerParams(vmem_limit_bytes=VMEM_LIMIT),
        name="mix_ffn_sample",
    )(x, ma, mb, wo, gf, wu, cw, cb, wd, gfin, st)


def _odd_kernel(x_ref, g_ref, w_ref, wg_ref, wgt_ref, bg_ref, bgt_ref, wp_ref, ps_ref,
                mix_ref, c_out, n_out, m_out, pool_out,
                c_scr, n_scr, m_scr, e_ref, s_a, s_b, hc_scr, kv_scr, *, tm):
    t = pl.program_id(1)
    top = 3 * SUBLANES

    @pl.when(t == 0)
    def _():
        c_scr[...] = jnp.zeros_like(c_scr)
        n_scr[...] = jnp.zeros_like(n_scr)
        m_scr[...] = jnp.zeros_like(m_scr)
        e_ref[0:top, :] = jnp.zeros((top, P_W), F32)
        s_a[0:SUBLANES, :] = jnp.zeros((SUBLANES, P_W), F32)
        s_b[0:SUBLANES, :] = jnp.zeros((SUBLANES, P_W), F32)

    h = _rms(x_ref[...], g_ref[...]).astype(BF)
    gz = _dot(h, wg_ref[...]) + bg_ref[...]
    gzt = _dot_nt(wgt_ref[...], h) + bgt_ref[...]
    lf_c = _log_sigmoid(gz)
    lf_r = _log_sigmoid(gzt)
    q = _dot(h, w_ref[:, 0:C_W]).astype(BF)
    kf = _dot(h, w_ref[:, C_W:2 * C_W]) * (DK_C ** -0.5)
    kb = kf.astype(BF)
    vb = _dot(h, w_ref[:, 2 * C_W:3 * C_W]).astype(BF)
    og = jax.nn.sigmoid(_dot(h, w_ref[:, 3 * C_W:4 * C_W]))
    p = _dot(h, w_ref[:, 4 * C_W:4 * C_W + P_W])

    tri_l = _tri(CHUNK, True)
    tri_u = _tri(CHUNK, False)
    r_i = lax.broadcasted_iota(jnp.int32, (CHUNK, CHUNK), 0)
    c_i = lax.broadcasted_iota(jnp.int32, (CHUNK, CHUNK), 1)
    causal = c_i <= r_i

    chunks = [slice(ci * CHUNK, (ci + 1) * CHUNK) for ci in range(tm // CHUNK)]
    heads = [slice(hd * DK_C, (hd + 1) * DK_C) for hd in range(H_C)]
    qk = {(ci, hd): _dot_nt(q[rows, lanes], kb[rows, lanes])
          for ci, rows in enumerate(chunks) for hd, lanes in enumerate(heads)}
    bcum_cs = [_tri_dot_left(tri_l, lf_c[rows, :]) for rows in chunks]
    bcum_rs = [_tri_dot_right(lf_r[:, rows], tri_u) for rows in chunks]
    stats, sc_b, wgk_t = {}, {}, {}
    for ci, rows in enumerate(chunks):
        bcum_c, bcum_r = bcum_cs[ci], bcum_rs[ci]
        for hd, lanes in enumerate(heads):
            bc = bcum_c[:, H_C + hd:H_C + hd + 1]
            br = bcum_r[H_C + hd:H_C + hd + 1, :]
            li_c = gz[rows, hd:hd + 1]
            li_r = gzt[hd:hd + 1, rows]
            dlog = jnp.where(causal, bc + (li_r - br), NEG)
            m_intra = jnp.max(dlog, axis=-1, keepdims=True)
            sc = qk[ci, hd] * jnp.exp(dlog - m_intra)
            bl = bc[CHUNK - 1:CHUNK, :]
            g_c = bl - bc + li_c
            g_max = jnp.max(g_c, axis=0, keepdims=True)
            wg = jnp.exp(g_c - g_max)
            kn = jnp.sum(_round_bf(wg) * kb[rows, lanes].astype(F32), axis=0, keepdims=True)
            stats[ci, hd] = (bc, m_intra, jnp.sum(sc, axis=-1, keepdims=True), bl, g_max, kn)
            sc_b[ci, hd] = sc.astype(BF)
            wgk_t[ci, hd] = (wg * kf[rows, lanes]).T.astype(BF)
    for ci, rows in enumerate(chunks):
        for hd, lanes in enumerate(heads):
            hc_scr[rows, lanes] = _dot(sc_b[ci, hd], vb[rows, lanes])
            kv_scr[ci, hd] = _dot(wgk_t[ci, hd], vb[rows, lanes])

    for ci, rows in enumerate(chunks):
        for hd, lanes in enumerate(heads):
            bc, m_intra, den_intra, bl, g_max, kn = stats[ci, hd]
            m_prev = m_scr[hd:hd + 1, 0:1]
            c_old = c_scr[hd]
            n_old = n_scr[hd:hd + 1, :]
            inter = bc + m_prev
            m_t = jnp.maximum(inter, m_intra)
            r_intra = jnp.exp(m_intra - m_t)
            w_inter = jnp.exp(inter - m_t)
            qh = q[rows, lanes]
            num = r_intra * hc_scr[rows, lanes] + w_inter * _dot(qh, c_old.astype(BF))
            qn = jnp.sum(qh.astype(F32) * _round_bf(n_old), axis=-1, keepdims=True)
            den = r_intra * den_intra + w_inter * qn
            hc_scr[rows, lanes] = num / jnp.maximum(jnp.abs(den), jnp.exp(-m_t))
            m_new = jnp.maximum(bl + m_prev, g_max)
            a = jnp.exp(bl + m_prev - m_new)
            r_state = jnp.exp(g_max - m_new)
            c_scr[hd] = a * c_old + r_state * kv_scr[ci, hd]
            n_scr[hd:hd + 1, :] = a * n_old + r_state * kn
            m_scr[hd:hd + 1, :] = jnp.broadcast_to(m_new, (1, LANES))

    mix_ref[:, 0:C_W] = (og * hc_scr[...]).astype(BF)

    n_rows = tm + top
    e_ref[top:n_rows, :] = p
    lo = SUBLANES
    s_a[lo:n_rows, :] = e_ref[lo:n_rows, :] + e_ref[lo - 1:n_rows - 1, :]
    s_b[lo:n_rows, :] = s_a[lo:n_rows, :] + s_a[lo - 2:n_rows - 2, :]
    w2 = s_a[top:n_rows, 0:POOL_G]
    s_a[lo:n_rows, :] = s_b[lo:n_rows, :] + s_b[lo - 4:n_rows - 4, :]
    w4 = s_b[top:n_rows, POOL_G:2 * POOL_G]
    w8 = s_a[top:n_rows, 2 * POOL_G:3 * POOL_G]
    w16 = s_a[top:n_rows, 3 * POOL_G:4 * POOL_G] + s_a[top - 8:n_rows - 8, 3 * POOL_G:4 * POOL_G]
    pos1 = (t * tm + lax.broadcasted_iota(jnp.int32, (tm, 1), 0) + 1).astype(F32)
    for gi, (win, wsum) in enumerate(zip(POOL_WINDOWS, (w2, w4, w8, w16))):
        lanes = slice(gi * POOL_G, (gi + 1) * POOL_G)
        mean = wsum / jnp.minimum(float(win), pos1)
        y = _dot((mean - p[:, lanes]).astype(BF), wp_ref[gi]) * ps_ref[:, lanes]
        mix_ref[:, C_W + gi * POOL_G:C_W + (gi + 1) * POOL_G] = y.astype(BF)
    pool_out[...] = e_ref[n_rows - 2 * SUBLANES:n_rows, :]
    e_ref[SUBLANES:top, :] = e_ref[n_rows - 2 * SUBLANES:n_rows, :]

    c_out[...] = c_scr[...]
    n_out[...] = n_scr[...]
    m_out[...] = m_scr[...]


def _odd_prompt(x, g, w_main, wg, wgt, bg, bgt, wp, ps, *, tm):
    b, s, _ = x.shape
    nt = s // tm
    state = lambda *shape: pl.BlockSpec((None,) + shape, lambda i, t: (i,) + (0,) * len(shape))
    return pl.pallas_call(
        functools.partial(_odd_kernel, tm=tm),
        grid=(b, nt),
        in_specs=[pl.BlockSpec((None, tm, D_MODEL), lambda i, t: (i, t, 0))]
                 + [_const_spec(a.shape) for a in (g, w_main, wg, wgt, bg, bgt, wp, ps)],
        out_specs=(pl.BlockSpec((None, tm, C_W + P_W), lambda i, t: (i, t, 0)),
                   state(H_C, DK_C, DK_C), state(SUBLANES, DK_C), state(SUBLANES, LANES),
                   state(2 * SUBLANES, P_W)),
        out_shape=(jax.ShapeDtypeStruct((b, s, C_W + P_W), BF),
                   jax.ShapeDtypeStruct((b, H_C, DK_C, DK_C), F32),
                   jax.ShapeDtypeStruct((b, SUBLANES, DK_C), F32),
                   jax.ShapeDtypeStruct((b, SUBLANES, LANES), F32),
                   jax.ShapeDtypeStruct((b, 2 * SUBLANES, P_W), F32)),
        scratch_shapes=[pltpu.VMEM((H_C, DK_C, DK_C), F32), pltpu.VMEM((SUBLANES, DK_C), F32),
                        pltpu.VMEM((SUBLANES, LANES), F32),
                        pltpu.VMEM((tm + 3 * SUBLANES, P_W), F32),
                        pltpu.VMEM((tm + 3 * SUBLANES, P_W), F32),
                        pltpu.VMEM((tm + 3 * SUBLANES, P_W), F32),
                        pltpu.VMEM((tm, C_W), F32),
                        pltpu.VMEM((tm // CHUNK, H_C, DK_C, DK_C), F32)],
        compiler_params=_params(("arbitrary", "arbitrary")),
        name="odd_prompt",
    )(x, g, w_main, wg, wgt, bg, bgt, wp, ps)


def _even_sample_kernel(x_ref, g_ref, w_ref, wf_ref, bf_ref, cw_ref, st_ref, seg_ref,
                        a_ref, q_ref, k_ref, v_ref, lf_ref, snew_ref, cn_ref):
    h = _rms(x_ref[...], g_ref[...]).astype(BF)
    u = _dot(h, w_ref[:, 0:A_W])
    gb = _dot(h, w_ref[:, A_W:2 * A_W])
    gc = _dot(h, w_ref[:, 2 * A_W:3 * A_W])
    cu = gc * u
    cw = cw_ref[...]
    prev1 = st_ref[:, A_W:2 * A_W]
    a_ref[...] = (gb * (cw[0:1] * st_ref[:, 0:A_W] + cw[1:2] * prev1 + cw[2:3] * cu)).astype(BF)
    cn_ref[:, 0:A_W] = prev1
    cn_ref[:, A_W:2 * A_W] = cu
    base = 3 * A_W
    qs = _round_bf(_dot(h, w_ref[:, base:base + B_W]) * (HD_B ** -0.5))
    k = _dot(h, w_ref[:, base + B_W:base + 2 * B_W])
    v = _dot(h, w_ref[:, base + 2 * B_W:base + 3 * B_W])
    k_ref[...] = k
    v_ref[...] = v
    lf_ref[...] = _log_sigmoid(_dot(h, wf_ref[...]) + bf_ref[...])[:, 0:H_B]
    snew_ref[...] = _tri_dot_right(qs * _round_bf(k), seg_ref[...])
    q_ref[...] = qs


def _even_sample(x, g, w_main, wf, bfl, cw, st, seg):
    n = x.shape[0]
    return pl.pallas_call(
        _even_sample_kernel,
        out_shape=(jax.ShapeDtypeStruct((n, A_W), BF),
                   jax.ShapeDtypeStruct((n, B_W), F32),
                   jax.ShapeDtypeStruct((n, B_W), F32),
                   jax.ShapeDtypeStruct((n, B_W), F32),
                   jax.ShapeDtypeStruct((n, H_B), F32),
                   jax.ShapeDtypeStruct((n, LANES), F32),
                   jax.ShapeDtypeStruct((n, 2 * A_W), F32)),
        compiler_params=pltpu.CompilerParams(vmem_limit_bytes=VMEM_LIMIT),
        name="even_sample",
    )(x, g, w_main, wf, bfl, cw, st, seg)


def _paged_kernel(pt_ref, qbd_ref, lfn_ref, snew_ref, vnew_ref, *refs, n_steps):
    npg = PAGES_PER_STEP
    k_refs = refs[0:npg]
    v_refs = refs[npg:2 * npg]
    f_refs = refs[2 * npg:3 * npg]
    o_ref = refs[3 * npg]
    m_scr, l_scr, acc_scr, suf_scr = refs[3 * npg + 1:]
    c = pl.program_id(1)
    own = (lax.broadcasted_iota(jnp.int32, (H_B, B_W), 1) // HD_B
           == lax.broadcasted_iota(jnp.int32, (H_B, B_W), 0))

    @pl.when(c == 0)
    def _():
        m_scr[...] = jnp.broadcast_to(snew_ref[...], m_scr.shape)
        l_scr[...] = jnp.ones_like(l_scr)
        acc_scr[...] = jnp.where(own, jnp.broadcast_to(_round_bf(vnew_ref[...]), (H_B, B_W)), 0.0)
        suf_scr[...] = jnp.broadcast_to(lfn_ref[...], suf_scr.shape)

    r_i = lax.broadcasted_iota(jnp.int32, (PAGE_SIZE, 2 * PAGE_SIZE), 0)
    c_i = lax.broadcasted_iota(jnp.int32, (PAGE_SIZE, 2 * PAGE_SIZE), 1)
    later_or_total = jnp.where((r_i > c_i) | (c_i >= PAGE_SIZE), 1.0, 0.0).astype(BF)
    lf_all = jnp.concatenate([f_refs[j][...] for j in range(npg)], axis=0)
    sums = _tri_dot_right(lf_all, later_or_total)

    q = qbd_ref[...]
    suf = suf_scr[:, 0:1]
    scores = []
    for j in reversed(range(npg)):
        rows = slice(j * H_B, (j + 1) * H_B)
        scores.append(_dot(q, k_refs[j][...].astype(BF)) + (sums[rows, 0:PAGE_SIZE] + suf))
        suf = suf + sums[rows, PAGE_SIZE:PAGE_SIZE + 1]
    suf_scr[...] = jnp.broadcast_to(suf, suf_scr.shape)
    m_old = m_scr[:, 0:1]
    m_new = m_old
    for s in scores:
        m_new = jnp.maximum(m_new, jnp.max(s, axis=-1, keepdims=True))
    alpha = jnp.exp(m_old - m_new)
    l = alpha * l_scr[:, 0:1]
    acc_scr[...] = alpha * acc_scr[...]
    for s, j in zip(scores, reversed(range(npg))):
        p = jnp.exp(s - m_new)
        l = l + jnp.sum(p, axis=-1, keepdims=True)
        acc_scr[...] += _dot_nt(p.astype(BF), v_refs[j][...].astype(BF))
    m_scr[...] = jnp.broadcast_to(m_new, m_scr.shape)
    l_scr[...] = jnp.broadcast_to(l, l_scr.shape)

    @pl.when(c == n_steps - 1)
    def _():
        o_ref[...] = jnp.sum(jnp.where(own, acc_scr[...] / l, 0.0), axis=0, keepdims=True)


def _paged_attention(page_table, qbd, lfn, snew, vnew, cache_k, cache_v, cache_lft):
    n, n_pages = page_table.shape
    npg = PAGES_PER_STEP
    n_steps = n_pages // npg

    def page_of(i, c, pt, j):
        return pt[i, (n_steps - 1 - c) * npg + j]

    per_sample = lambda *shape: pl.BlockSpec((None,) + shape, lambda i, c, pt: (i,) + (0,) * len(shape))
    kv_spec = lambda j: pl.BlockSpec((None, B_W, PAGE_SIZE), lambda i, c, pt: (page_of(i, c, pt, j), 0, 0))
    lf_spec = lambda j: pl.BlockSpec((None, H_B, PAGE_SIZE), lambda i, c, pt: (page_of(i, c, pt, j), 0, 0))
    in_specs = ([per_sample(H_B, B_W), per_sample(H_B, 1), per_sample(H_B, 1), per_sample(1, B_W)]
                + [kv_spec(j) for j in range(npg)] + [kv_spec(j) for j in range(npg)]
                + [lf_spec(j) for j in range(npg)])
    grid_spec = pltpu.PrefetchScalarGridSpec(
        num_scalar_prefetch=1, grid=(n, n_steps), in_specs=in_specs,
        out_specs=per_sample(1, B_W),
        scratch_shapes=[pltpu.VMEM((H_B, LANES), F32), pltpu.VMEM((H_B, LANES), F32),
                        pltpu.VMEM((H_B, B_W), F32), pltpu.VMEM((H_B, LANES), F32)])
    return pl.pallas_call(
        functools.partial(_paged_kernel, n_steps=n_steps),
        grid_spec=grid_spec,
        out_shape=jax.ShapeDtypeStruct((n, 1, B_W), F32),
        compiler_params=_params(("arbitrary", "arbitrary")),
        name="paged_attention",
    )(page_table, qbd, lfn, snew, vnew, *([cache_k] * npg), *([cache_v] * npg), *([cache_lft] * npg))


def _odd_sample_kernel(x_ref, g_ref, w_ref, wg_ref, bg_ref, wp_ref, ps_ref, c_ref, n_ref, m_ref, pool_ref,
                       mix_ref, c_out, n_out, m_out, pool_out, *, pool_div):
    nb = x_ref.shape[0]
    h = _rms(x_ref[...], g_ref[...]).astype(BF)
    q = _round_bf(_dot(h, w_ref[:, 0:C_W]))
    kf = _dot(h, w_ref[:, C_W:2 * C_W]) * (DK_C ** -0.5)
    k = _round_bf(kf)
    v = _round_bf(_dot(h, w_ref[:, 2 * C_W:3 * C_W]))
    og = jax.nn.sigmoid(_dot(h, w_ref[:, 3 * C_W:4 * C_W]))
    p = _dot(h, w_ref[:, 4 * C_W:4 * C_W + P_W])
    gz = _dot(h, wg_ref[...]) + bg_ref[...]
    lf = _log_sigmoid(gz)
    eye = (lax.broadcasted_iota(jnp.int32, (DK_C, DK_C), 0)
           == lax.broadcasted_iota(jnp.int32, (DK_C, DK_C), 1))

    for hd in range(H_C):
        lanes = slice(hd * DK_C, (hd + 1) * DK_C)
        li = gz[:, hd:hd + 1]
        lfh = lf[:, H_C + hd:H_C + hd + 1]
        m_prev = m_ref[:, hd:hd + 1]
        qh = q[:, lanes]
        kh = k[:, lanes]
        vh = v[:, lanes]
        n_old = n_ref[:, lanes]
        m_t = jnp.maximum(lfh + m_prev, li)
        sc = jnp.sum(qh * kh, axis=-1, keepdims=True) * jnp.exp(li - m_t)
        w_inter = jnp.exp(lfh + m_prev - m_t)
        qc = jnp.concatenate(
            [_dot(qh[i:i + 1, :].astype(BF), c_ref[i, hd].astype(BF)) for i in range(nb)], axis=0)
        num = _round_bf(sc) * vh + w_inter * qc
        den = sc + w_inter * jnp.sum(qh * _round_bf(n_old), axis=-1, keepdims=True)
        hout = num / jnp.maximum(jnp.abs(den), jnp.exp(-m_t))
        mix_ref[:, lanes] = (og[:, lanes] * hout).astype(BF)
        a = w_inter
        wg = jnp.exp(li - m_t)
        wgk = wg * kf[:, lanes]
        n_out[:, lanes] = a * n_old + _round_bf(wg) * kh
        m_out[:, hd:hd + 1] = m_t
        for i in range(nb):
            kd = jnp.where(eye, jnp.broadcast_to(wgk[i:i + 1, :], (DK_C, DK_C)), 0.0).astype(BF)
            vrep = jnp.broadcast_to(vh[i:i + 1, :], (DK_C, DK_C)).astype(BF)
            c_out[i, hd] = a[i:i + 1, :] * c_ref[i, hd] + _dot(kd, vrep)

    prev = pool_ref[...]
    row = lax.broadcasted_iota(jnp.int32, prev.shape, 1)
    for gi, win in enumerate(POOL_WINDOWS):
        lanes = slice(gi * POOL_G, (gi + 1) * POOL_G)
        tail = jnp.sum(jnp.where(row >= POOL_PREV - (win - 1), prev, 0.0)[:, :, lanes], axis=1)
        mean = (tail + p[:, lanes]) / pool_div[gi]
        y = _dot((mean - p[:, lanes]).astype(BF), wp_ref[gi]) * ps_ref[:, lanes]
        mix_ref[:, C_W + gi * POOL_G:C_W + (gi + 1) * POOL_G] = y.astype(BF)
    pool_out[:, 0:POOL_PREV - 1, :] = pool_ref[:, 1:POOL_PREV, :]
    for i in range(nb):
        pool_out[i, POOL_PREV - 1:POOL_PREV, :] = p[i:i + 1, :]


def _odd_sample(x, g, w_main, wg, bg, wp, ps, c, n, m, pool, *, pool_div, nb):
    nsmp = x.shape[0]
    rows = lambda w: pl.BlockSpec((nb, w), lambda i: (i, 0))
    return pl.pallas_call(
        functools.partial(_odd_sample_kernel, pool_div=pool_div),
        grid=(nsmp // nb,),
        in_specs=[rows(D_MODEL)] + [_const_spec(a.shape) for a in (g, w_main, wg, bg, wp, ps)]
                 + [pl.BlockSpec((nb, H_C, DK_C, DK_C), lambda i: (i, 0, 0, 0)),
                    rows(C_W), rows(H_C),
                    pl.BlockSpec((nb, POOL_PREV, P_W), lambda i: (i, 0, 0))],
        out_specs=(rows(C_W + P_W),
                   pl.BlockSpec((nb, H_C, DK_C, DK_C), lambda i: (i, 0, 0, 0)),
                   rows(C_W), rows(H_C),
                   pl.BlockSpec((nb, POOL_PREV, P_W), lambda i: (i, 0, 0))),
        out_shape=(jax.ShapeDtypeStruct((nsmp, C_W + P_W), BF),
                   jax.ShapeDtypeStruct((nsmp, H_C, DK_C, DK_C), F32),
                   jax.ShapeDtypeStruct((nsmp, C_W), F32),
                   jax.ShapeDtypeStruct((nsmp, H_C), F32),
                   jax.ShapeDtypeStruct((nsmp, POOL_PREV, P_W), F32)),
        compiler_params=_params(("arbitrary",)),
        name="odd_sample",
    )(x, g, w_main, wg, bg, wp, ps, c, n, m, pool)


def _pad_cols(w, n):
    return jnp.pad(w, ((0, 0), (0, n - w.shape[1])))


def _ffn_weights(w_up, conv_w, conv_b, w_down):
    return w_up.astype(BF), conv_w, conv_b.reshape(1, D_FF), w_down.astype(BF)


def kernel(x_prompt, x_sample, cache_k, cache_v, cache_logf, state_conv_a, state_mlstm_c, state_mlstm_n,
           state_mlstm_m, state_pool, state_ffn_conv, page_table, norm_mix, norm_ffn, norm_final, w_in_even,
           b_forget_even, conv_a, w_out_even, w_in_odd, b_igate_odd, b_fgate_odd, w_pool_odd, pool_scale_odd,
           w_out_odd, w_up, ffn_conv_w, ffn_conv_b, w_down):
    b, s, _ = x_prompt.shape
    nsmp = x_sample.shape[0]
    n_pages = page_table.shape[1]
    assert x_sample.shape[1] == 1 and norm_mix.shape[0] == 2
    tm = min(512, s)
    tq = min(256, s)
    assert s % tm == 0 and n_pages % PAGES_PER_STEP == 0

    xs = x_sample.reshape(nsmp, D_MODEL)
    row = lambda v: v.reshape(1, -1)
    g_final = row(norm_final)

    w = w_in_even[0]
    n_main = 3 * A_W + 3 * B_W
    w_main = w[:, :n_main].astype(BF)
    wf = _pad_cols(w[:, n_main:], LANES).astype(BF)
    wft = jnp.pad(w[:, n_main:].T, ((0, 2 * SUBLANES - H_B), (0, 0))).astype(BF)
    bfl = _pad_cols(row(b_forget_even[0]), LANES)
    bft = jnp.pad(b_forget_even[0].reshape(H_B, 1), ((0, 2 * SUBLANES - H_B), (0, 0)))
    g_mix = row(norm_mix[0])
    wo = w_out_even[0].astype(BF)
    ffn0 = _ffn_weights(w_up[0], ffn_conv_w[0], ffn_conv_b[0], w_down[0])

    wqt = w[:, 3 * A_W:3 * A_W + B_W].T.astype(BF)
    wvt = w[:, 3 * A_W + 2 * B_W:n_main].T.astype(BF)
    a_p, qt_p, k_p, v_p, kb_p, vt_p, lf_p, fc_p, fr_p, ca_p = _even_proj(
        x_prompt, g_mix, w_main, wqt, wvt, wf, wft, bfl, bft, conv_a[0], tm=tm, tk=tq)
    att_p = _fox_prompt(qt_p, kb_p, vt_p, fc_p, fr_p, tq=tq)
    xp, ff0_p = _mix_ffn(x_prompt, a_p, att_p, 0, wo, row(norm_ffn[0]), *ffn0, g_final, tm=tm, final_norm=False)

    seg = jnp.asarray((np.arange(B_W)[:, None] // HD_B == np.arange(LANES)[None, :]).astype(np.float32), dtype=BF)
    a_s, q_s, k_s, v_s, lf_s, snew_s, ca_s = _even_sample(
        xs, g_mix, w_main, wf, bfl, conv_a[0], state_conv_a[0].reshape(nsmp, 2 * A_W), seg)
    head_cols = jnp.asarray(np.arange(B_W)[None, :] // HD_B == np.arange(H_B)[:, None])
    qbd = jnp.where(head_cols[None], q_s[:, None, :], 0.0).astype(BF)
    n_phys = cache_k.shape[1]
    kt = cache_k[0].transpose(0, 2, 3, 1).reshape(n_phys, B_W, PAGE_SIZE)
    vt = cache_v[0].transpose(0, 2, 3, 1).reshape(n_phys, B_W, PAGE_SIZE)
    att_s = _paged_attention(
        page_table, qbd, lf_s.reshape(nsmp, H_B, 1), snew_s[:, :H_B].reshape(nsmp, H_B, 1),
        v_s.reshape(nsmp, 1, B_W), kt, vt, cache_logf[0].transpose(0, 2, 1))
    xs1, ff0_s = _mix_ffn_sample(xs, a_s, att_s.reshape(nsmp, B_W).astype(BF), wo, row(norm_ffn[0]), *ffn0,
                                 g_final, state_ffn_conv[0].reshape(nsmp, 2 * D_FF), final_norm=False)

    w = w_in_odd[0]
    n_qkvo = 4 * C_W
    w_main = jnp.concatenate([w[:, :n_qkvo], w[:, n_qkvo + 2 * H_C:]], axis=1).astype(BF)
    w_gate = w[:, n_qkvo:n_qkvo + 2 * H_C]
    wg = _pad_cols(w_gate, LANES).astype(BF)
    wgt = jnp.pad(w_gate.T, ((0, 2 * SUBLANES - 2 * H_C), (0, 0))).astype(BF)
    b_gate = jnp.concatenate([b_igate_odd[0], b_fgate_odd[0]])
    bg = _pad_cols(row(b_gate), LANES)
    bgt = jnp.pad(b_gate.reshape(2 * H_C, 1), ((0, 2 * SUBLANES - 2 * H_C), (0, 0)))
    g_mix = row(norm_mix[1])
    wp = w_pool_odd[0].astype(BF)
    ps = row(pool_scale_odd[0])
    wo = w_out_odd[0].astype(BF)
    ffn1 = _ffn_weights(w_up[1], ffn_conv_w[1], ffn_conv_b[1], w_down[1])

    mix_p, c_p, n_p, m_p, pool_p = _odd_prompt(xp, g_mix, w_main, wg, wgt, bg, bgt, wp, ps, tm=tm)
    y_p, ff1_p = _mix_ffn(xp, mix_p, mix_p, 1, wo, row(norm_ffn[1]), *ffn1, g_final, tm=tm, final_norm=True)

    pos0 = n_pages * PAGE_SIZE
    pool_div = tuple(float(min(win, pos0 + 1)) for win in POOL_WINDOWS)
    mix_s, c_s, n_s, m_s, pool_s = _odd_sample(
        xs1, g_mix, w_main, wg, bg, wp, ps, state_mlstm_c[0], state_mlstm_n[0].reshape(nsmp, C_W),
        state_mlstm_m[0], state_pool[0], pool_div=pool_div, nb=min(8, nsmp))
    y_s, ff1_s = _mix_ffn_sample(xs1, mix_s[:, :C_W], mix_s[:, C_W:], wo, row(norm_ffn[1]), *ffn1, g_final,
                                 state_ffn_conv[1].reshape(nsmp, 2 * D_FF), final_norm=True)

    heads = lambda z: z.reshape(z.shape[:-1] + (H_B, HD_B))
    return (
        y_p, y_s.reshape(nsmp, 1, D_MODEL),
        heads(k_p)[None], heads(k_s).reshape(1, nsmp, 1, H_B, HD_B),
        heads(v_p)[None], heads(v_s).reshape(1, nsmp, 1, H_B, HD_B),
        lf_p[None], lf_s.reshape(1, nsmp, 1, H_B),
        ca_p[None], ca_s.reshape(1, nsmp, 2, A_W),
        c_p[None], c_s[None],
        n_p[:, :H_C][None], n_s.reshape(1, nsmp, H_C, DK_C),
        m_p[:, :H_C, 0][None], m_s[None],
        pool_p[:, 1:][None], pool_s[None],
        jnp.stack([ff0_p, ff1_p]),
        jnp.stack([ff0_s.reshape(nsmp, 2, D_FF), ff1_s.reshape(nsmp, 2, D_FF)]),
    )
```

```python
import functools

import numpy as np
import jax
import jax.numpy as jnp
from jax import lax
from jax.experimental import pallas as pl
from jax.experimental.pallas import tpu as pltpu

D_MODEL = 1024
A_W = 512
B_W = 512
H_B = 8
HD_B = 64
C_W = 512
H_C = 4
DK_C = 128
P_W = 512
POOL_WINDOWS = (2, 4, 8, 16)
POOL_G = 128
POOL_PREV = 15
D_FF = 2816
PAGE_SIZE = 128
CHUNK = 128
EPS = 1e-6

LANES = 128
SUBLANES = 8
FF_CHUNK = 256
N_FF = D_FF // FF_CHUNK
PAGES_PER_STEP = 16
VMEM_LIMIT = 56 * 1024 * 1024
NEG = -1e30

BF = jnp.bfloat16
F32 = jnp.float32


def _rms(x, g):
    return x * lax.rsqrt(jnp.mean(x * x, axis=-1, keepdims=True) + EPS) * g


def _dot(a, b):
    return jnp.dot(a, b, preferred_element_type=F32)


def _dot_nt(a, b):
    return lax.dot_general(a, b, (((1,), (1,)), ((), ())), preferred_element_type=F32)


def _split3(x):
    hi = x.astype(BF)
    r = x - hi.astype(F32)
    mid = r.astype(BF)
    lo = (r - mid.astype(F32)).astype(BF)
    return hi, mid, lo


def _tri_dot_left(tri, x):
    hi, mid, lo = _split3(x)
    return _dot(tri, hi) + _dot(tri, mid) + _dot(tri, lo)


def _tri_dot_right(x, tri):
    hi, mid, lo = _split3(x)
    return _dot(hi, tri) + _dot(mid, tri) + _dot(lo, tri)


def _log_sigmoid(x):
    return jnp.minimum(x, 0.0) - jnp.log1p(jnp.exp(-jnp.abs(x)))


def _tri(n, lower):
    r = lax.broadcasted_iota(jnp.int32, (n, n), 0)
    c = lax.broadcasted_iota(jnp.int32, (n, n), 1)
    keep = (c <= r) if lower else (r <= c)
    return jnp.where(keep, 1.0, 0.0).astype(BF)


def _round_bf(x):
    return x.astype(BF).astype(F32)


def _const_spec(shape):
    zeros = (0,) * len(shape)
    return pl.BlockSpec(shape, lambda *_: zeros, pipeline_mode=pl.Buffered(1))


def _layer_spec(shape, layer):
    zeros = (0,) * (len(shape) - 1)
    return pl.BlockSpec((None,) + tuple(shape[1:]), lambda *_: (layer,) + zeros, pipeline_mode=pl.Buffered(1))


def _params(sem):
    return pltpu.CompilerParams(dimension_semantics=sem, vmem_limit_bytes=VMEM_LIMIT)


def _even_proj_kernel(x_ref, g_ref, w_ref, wqt_ref, wvt_ref, wf_ref, wft_ref, bf_ref, bft_ref, cw_ref,
                      a_ref, qt_ref, k_ref, v_ref, kb_ref, vt_ref, lf_ref, fc_ref, fr_ref, cn_ref,
                      ext_ref, ccol_ref, crow_ref, *, tm, tk):
    t = pl.program_id(1)

    @pl.when(t == 0)
    def _():
        ext_ref[0:SUBLANES, :] = jnp.zeros((SUBLANES, A_W), F32)
        ccol_ref[...] = jnp.zeros_like(ccol_ref)
        crow_ref[...] = jnp.zeros_like(crow_ref)

    h = _rms(x_ref[...], g_ref[...]).astype(BF)
    lf = _log_sigmoid(_dot(h, wf_ref[...]) + bf_ref[...])
    lft = _log_sigmoid(_dot_nt(wft_ref[...], h) + bft_ref[...])
    lf_ref[...] = lf[:, 0:H_B]

    u = _dot(h, w_ref[:, 0:A_W])
    gb = _dot(h, w_ref[:, A_W:2 * A_W])
    gc = _dot(h, w_ref[:, 2 * A_W:3 * A_W])

    cs = _tri_dot_left(_tri(tm, True), lf) + ccol_ref[0:1, :]
    fc_ref[...] = cs[:, 0:H_B]
    ccol_ref[...] = jnp.broadcast_to(cs[tm - 1:tm, :], ccol_ref.shape)
    cst = _tri_dot_right(lft, _tri(tm, False)) + crow_ref[:, 0:1]
    for j in range(tm // tk):
        fr_ref[j] = cst[0:H_B, j * tk:(j + 1) * tk]
    crow_ref[...] = jnp.broadcast_to(cst[:, tm - 1:tm], crow_ref.shape)

    cu = gc * u
    ext_ref[SUBLANES:SUBLANES + tm, :] = cu
    cw = cw_ref[...]
    conv = (cw[0:1] * ext_ref[SUBLANES - 2:SUBLANES - 2 + tm, :]
            + cw[1:2] * ext_ref[SUBLANES - 1:SUBLANES - 1 + tm, :]
            + cw[2:3] * cu)
    a_ref[...] = (gb * conv).astype(BF)
    cn_ref[...] = ext_ref[tm + SUBLANES - 2:tm + SUBLANES, :]
    ext_ref[0:SUBLANES, :] = ext_ref[tm:tm + SUBLANES, :]

    base = 3 * A_W
    k = _dot(h, w_ref[:, base + B_W:base + 2 * B_W])
    k_ref[...] = k
    kb_ref[...] = k.astype(BF)
    v_ref[...] = _dot(h, w_ref[:, base + 2 * B_W:base + 3 * B_W])
    qt = (_dot_nt(wqt_ref[...], h) * (HD_B ** -0.5)).astype(BF)
    vt = _dot_nt(wvt_ref[...], h).astype(BF)
    for j in range(tm // tk):
        qt_ref[j] = qt[:, j * tk:(j + 1) * tk]
        vt_ref[j] = vt[:, j * tk:(j + 1) * tk]


def _even_proj(x, g, w_main, wqt, wvt, wf, wft, bfl, bft, cw, *, tm, tk):
    b, s, _ = x.shape
    nt = s // tm
    row = lambda w: pl.BlockSpec((None, tm, w), lambda i, t: (i, t, 0))
    chunked = lambda rows: pl.BlockSpec((None, tm // tk, rows, tk), lambda i, t: (i, t, 0, 0))
    out_shape = (
        jax.ShapeDtypeStruct((b, s, A_W), BF),
        jax.ShapeDtypeStruct((b, s // tk, B_W, tk), BF),
        jax.ShapeDtypeStruct((b, s, B_W), F32),
        jax.ShapeDtypeStruct((b, s, B_W), F32),
        jax.ShapeDtypeStruct((b, s, B_W), BF),
        jax.ShapeDtypeStruct((b, s // tk, B_W, tk), BF),
        jax.ShapeDtypeStruct((b, s, H_B), F32),
        jax.ShapeDtypeStruct((b, s, H_B), F32),
        jax.ShapeDtypeStruct((b, s // tk, H_B, tk), F32),
        jax.ShapeDtypeStruct((b, 2, A_W), F32),
    )
    out_specs = (
        row(A_W), chunked(B_W), row(B_W), row(B_W), row(B_W), chunked(B_W), row(H_B), row(H_B),
        chunked(H_B),
        pl.BlockSpec((None, 2, A_W), lambda i, t: (i, 0, 0)),
    )
    in_specs = [row(D_MODEL)] + [_const_spec(a.shape) for a in (g, w_main, wqt, wvt, wf, wft, bfl, bft, cw)]
    return pl.pallas_call(
        functools.partial(_even_proj_kernel, tm=tm, tk=tk),
        grid=(b, nt), in_specs=in_specs, out_specs=out_specs, out_shape=out_shape,
        scratch_shapes=[pltpu.VMEM((tm + SUBLANES, A_W), F32),
                        pltpu.VMEM((SUBLANES, LANES), F32),
                        pltpu.VMEM((2 * SUBLANES, LANES), F32)],
        compiler_params=_params(("arbitrary", "arbitrary")),
        name="even_proj",
    )(x, g, w_main, wqt, wvt, wf, wft, bfl, bft, cw)


def _fox_kernel(qt_ref, k_ref, vt_ref, fc_ref, fr_ref, o_ref, qm_scr, m_scr, l_scr, acc_scr, *, tq):
    qi = pl.program_id(1)
    sub = lax.broadcasted_iota(jnp.int32, (LANES, tq), 0)
    for h in range(H_B):
        hp, hh = divmod(h, 2)
        qp = qt_ref[hp * LANES:(hp + 1) * LANES, :]
        in_head = (sub < HD_B) if hh == 0 else (sub >= HD_B)
        qm_scr[h] = jnp.where(in_head, qp, jnp.zeros_like(qp))
    m_scr[...] = jnp.full(m_scr.shape, NEG, F32)
    l_scr[...] = jnp.zeros_like(l_scr)
    acc_scr[...] = jnp.zeros_like(acc_scr)
    causal = (lax.broadcasted_iota(jnp.int32, (tq, tq), 0) <= lax.broadcasted_iota(jnp.int32, (tq, tq), 1))

    def step(j, masked):
        off = pl.multiple_of(j * tq, tq)
        group = lambda h: slice((h // 2) * LANES, (h // 2 + 1) * LANES)
        scores = [_dot(k_ref[pl.ds(off, tq), group(h)], qm_scr[h]) for h in range(H_B)]
        probs, alphas = [], []
        for h in range(H_B):
            s = scores[h] - fc_ref[pl.ds(off, tq), h:h + 1]
            if masked:
                s = jnp.where(causal, s, NEG)
            f_q = fr_ref[h:h + 1, :]
            m_old = m_scr[h:h + 1, :]
            m_new = jnp.maximum(m_old, jnp.max(s, axis=0, keepdims=True) + f_q)
            p = jnp.exp(s - (m_new - f_q))
            alpha = jnp.exp(m_old - m_new)
            l_scr[h:h + 1, :] = alpha * l_scr[h:h + 1, :] + jnp.sum(p, axis=0, keepdims=True)
            m_scr[h:h + 1, :] = m_new
            probs.append(p.astype(BF))
            alphas.append(alpha)
        for h in range(H_B):
            hp, hh = divmod(h, 2)
            rows = slice(hh * HD_B, (hh + 1) * HD_B)
            pv = _dot(vt_ref[j, group(h), :], probs[h])
            acc_scr[hp, rows, :] = alphas[h] * acc_scr[hp, rows, :] + pv[rows, :]

    def body(j, carry):
        step(j, False)
        return carry

    lax.fori_loop(0, qi, body, 0)
    step(qi, True)
    for hp in range(H_B // 2):
        denom = jnp.where(sub < HD_B, l_scr[2 * hp:2 * hp + 1, :], l_scr[2 * hp + 1:2 * hp + 2, :])
        o_ref[:, hp * LANES:(hp + 1) * LANES] = (acc_scr[hp] / denom).T.astype(BF)


def _fox_prompt(qt, kb, vt, fc, fr, *, tq):
    b, nq, _, _ = qt.shape
    s = nq * tq
    return pl.pallas_call(
        functools.partial(_fox_kernel, tq=tq),
        grid=(b, nq),
        in_specs=[
            pl.BlockSpec((None, None, B_W, tq), lambda i, t: (i, t, 0, 0)),
            pl.BlockSpec((None, s, B_W), lambda i, t: (i, 0, 0)),
            pl.BlockSpec((None, nq, B_W, tq), lambda i, t: (i, 0, 0, 0)),
            pl.BlockSpec((None, s, H_B), lambda i, t: (i, 0, 0)),
            pl.BlockSpec((None, None, H_B, tq), lambda i, t: (i, t, 0, 0)),
        ],
        out_specs=pl.BlockSpec((None, tq, B_W), lambda i, t: (i, t, 0)),
        out_shape=jax.ShapeDtypeStruct((b, s, B_W), BF),
        scratch_shapes=[pltpu.VMEM((H_B, LANES, tq), BF), pltpu.VMEM((H_B, tq), F32),
                        pltpu.VMEM((H_B, tq), F32), pltpu.VMEM((H_B // 2, LANES, tq), F32)],
        compiler_params=_params(("arbitrary", "arbitrary")),
        name="fox_prompt",
    )(qt, kb, vt, fc, fr)


def _mix_ffn_kernel(x_ref, ma_ref, mb_ref, wo_ref, gf_ref, wu_ref, cw_ref, cb_ref, wd_ref, gfin_ref,
                    y_ref, st_ref, acc_ref, h_ref, act_ref, ext_ref, carry_ref, *, tm, final_norm):
    t = pl.program_id(1)

    @pl.when(t == 0)
    def _():
        carry_ref[...] = jnp.zeros_like(carry_ref)

    half = wo_ref.shape[0] // 2
    x1 = x_ref[...] + _dot(ma_ref[...], wo_ref[0:half, :]) + _dot(mb_ref[...], wo_ref[half:2 * half, :])
    acc_ref[...] = x1
    h_ref[...] = _rms(x1, gf_ref[...]).astype(BF)

    def up(c):
        gate_cols = slice(c * FF_CHUNK, (c + 1) * FF_CHUNK)
        lin_cols = slice(D_FF + c * FF_CHUNK, D_FF + (c + 1) * FF_CHUNK)
        return _dot(h_ref[...], wu_ref[:, gate_cols]), _dot(h_ref[...], wu_ref[:, lin_cols])

    nxt = up(0)
    for c in range(N_FF):
        cols = slice(c * FF_CHUNK, (c + 1) * FF_CHUNK)
        g, u = nxt
        if c + 1 < N_FF:
            nxt = up(c + 1)
        ext = ext_ref.at[c % 2]
        ext[0:SUBLANES, :] = carry_ref[c]
        ext[SUBLANES:SUBLANES + tm, :] = g
        cw = cw_ref[:, cols]
        gconv = (cw[0:1] * ext[SUBLANES - 2:SUBLANES - 2 + tm, :]
                 + cw[1:2] * ext[SUBLANES - 1:SUBLANES - 1 + tm, :]
                 + cw[2:3] * g + cb_ref[:, cols])
        act_ref[:, cols] = (gconv * jax.nn.sigmoid(gconv) * u).astype(BF)
        carry_ref[c] = ext[tm:tm + SUBLANES, :]
        st_ref[:, cols] = ext[tm + SUBLANES - 2:tm + SUBLANES, :]
    y = acc_ref[...] + _dot(act_ref[...], wd_ref[...])
    y_ref[...] = _rms(y, gfin_ref[...]) if final_norm else y


def _mix_ffn(x, ma, mb, lane_blk_b, wo, gf, wu, cw, cb, wd, gfin, *, layer, tm, final_norm):
    b, s, _ = x.shape
    nt = s // tm
    half = wo.shape[0] // 2
    return pl.pallas_call(
        functools.partial(_mix_ffn_kernel, tm=tm, final_norm=final_norm),
        grid=(b, nt),
        in_specs=[
            pl.BlockSpec((None, tm, D_MODEL), lambda i, t: (i, t, 0)),
            pl.BlockSpec((None, tm, half), lambda i, t: (i, t, 0)),
            pl.BlockSpec((None, tm, half), lambda i, t: (i, t, lane_blk_b)),
            _const_spec(wo.shape), _const_spec(gf.shape), _layer_spec(wu.shape, layer),
            _const_spec(cw.shape), _const_spec(cb.shape), _layer_spec(wd.shape, layer), _const_spec(gfin.shape),
        ],
        out_specs=(pl.BlockSpec((None, tm, D_MODEL), lambda i, t: (i, t, 0)),
                   pl.BlockSpec((None, 2, D_FF), lambda i, t: (i, 0, 0))),
        out_shape=(jax.ShapeDtypeStruct((b, s, D_MODEL), F32),
                   jax.ShapeDtypeStruct((b, 2, D_FF), F32)),
        scratch_shapes=[pltpu.VMEM((tm, D_MODEL), F32), pltpu.VMEM((tm, D_MODEL), BF),
                        pltpu.VMEM((tm, D_FF), BF),
                        pltpu.VMEM((2, tm + SUBLANES, FF_CHUNK), F32),
                        pltpu.VMEM((N_FF, SUBLANES, FF_CHUNK), F32)],
        compiler_params=_params(("arbitrary", "arbitrary")),
        name="mix_ffn",
    )(x, ma, mb, wo, gf, wu, cw, cb, wd, gfin)


def _mix_ffn_sample_kernel(x_ref, ma_ref, mb_ref, wo_ref, gf_ref, wu_ref, cw_ref, cb_ref, wd_ref,
                           gfin_ref, st_ref, y_ref, stn_ref, *, final_norm):
    half = wo_ref.shape[0] // 2
    x1 = x_ref[...] + _dot(ma_ref[...], wo_ref[0:half, :]) + _dot(mb_ref[...], wo_ref[half:2 * half, :])
    h = _rms(x1, gf_ref[...]).astype(BF)
    y_ref[...] = x1
    for c in range(N_FF):
        cols = slice(c * FF_CHUNK, (c + 1) * FF_CHUNK)
        cols1 = slice(D_FF + c * FF_CHUNK, D_FF + (c + 1) * FF_CHUNK)
        g = _dot(h, wu_ref[:, cols])
        u = _dot(h, wu_ref[:, cols1])
        cw = cw_ref[:, cols]
        prev1 = st_ref[:, cols1]
        gconv = cw[0:1] * st_ref[:, cols] + cw[1:2] * prev1 + cw[2:3] * g + cb_ref[:, cols]
        act = (gconv * jax.nn.sigmoid(gconv) * u).astype(BF)
        y_ref[...] += _dot(act, wd_ref[cols, :])
        stn_ref[:, cols] = prev1
        stn_ref[:, cols1] = g
    if final_norm:
        y_ref[...] = _rms(y_ref[...], gfin_ref[...])


def _mix_ffn_sample(x, ma, mb, wo, gf, wu, cw, cb, wd, gfin, st, *, layer, final_norm):
    n = x.shape[0]
    whole = lambda a: _const_spec(a.shape)
    out_shape = (jax.ShapeDtypeStruct((n, D_MODEL), F32), jax.ShapeDtypeStruct((n, 2 * D_FF), F32))
    return pl.pallas_call(
        functools.partial(_mix_ffn_sample_kernel, final_norm=final_norm),
        grid=(1,),
        in_specs=[whole(x), whole(ma), whole(mb), whole(wo), whole(gf), _layer_spec(wu.shape, layer), whole(cw),
                  whole(cb), _layer_spec(wd.shape, layer), whole(gfin), whole(st)],
        out_specs=tuple(pl.BlockSpec(o.shape, lambda i: (0, 0)) for o in out_shape),
        out_shape=out_shape,
        compiler_params=_params(("arbitrary",)),
        name="mix_ffn_sample",
    )(x, ma, mb, wo, gf, wu, cw, cb, wd, gfin, st)


def _odd_kernel(x_ref, g_ref, w_ref, wg_ref, wgt_ref, bg_ref, bgt_ref, wp_ref, ps_ref,
                mix_ref, c_out, n_out, m_out, pool_out,
                c_scr, n_scr, m_scr, e_ref, s_a, s_b, hc_scr, kv_scr, *, tm):
    t = pl.program_id(1)
    top = 3 * SUBLANES

    @pl.when(t == 0)
    def _():
        c_scr[...] = jnp.zeros_like(c_scr)
        n_scr[...] = jnp.zeros_like(n_scr)
        m_scr[...] = jnp.zeros_like(m_scr)
        e_ref[0:top, :] = jnp.zeros((top, P_W), F32)
        s_a[0:SUBLANES, :] = jnp.zeros((SUBLANES, P_W), F32)
        s_b[0:SUBLANES, :] = jnp.zeros((SUBLANES, P_W), F32)

    h = _rms(x_ref[...], g_ref[...]).astype(BF)
    gz = _dot(h, wg_ref[...]) + bg_ref[...]
    gzt = _dot_nt(wgt_ref[...], h) + bgt_ref[...]
    lf_c = _log_sigmoid(gz)
    lf_r = _log_sigmoid(gzt)
    q = _dot(h, w_ref[:, 0:C_W]).astype(BF)
    kf = _dot(h, w_ref[:, C_W:2 * C_W]) * (DK_C ** -0.5)
    kb = kf.astype(BF)
    vb = _dot(h, w_ref[:, 2 * C_W:3 * C_W]).astype(BF)
    og = jax.nn.sigmoid(_dot(h, w_ref[:, 3 * C_W:4 * C_W]))
    p = _dot(h, w_ref[:, 4 * C_W:4 * C_W + P_W])

    tri_l = _tri(CHUNK, True)
    tri_u = _tri(CHUNK, False)
    r_i = lax.broadcasted_iota(jnp.int32, (CHUNK, CHUNK), 0)
    c_i = lax.broadcasted_iota(jnp.int32, (CHUNK, CHUNK), 1)
    causal = c_i <= r_i

    chunks = [slice(ci * CHUNK, (ci + 1) * CHUNK) for ci in range(tm // CHUNK)]
    heads = [slice(hd * DK_C, (hd + 1) * DK_C) for hd in range(H_C)]
    qk = {(ci, hd): _dot_nt(q[rows, lanes], kb[rows, lanes])
          for ci, rows in enumerate(chunks) for hd, lanes in enumerate(heads)}
    bcum_cs = [_tri_dot_left(tri_l, lf_c[rows, :]) for rows in chunks]
    bcum_rs = [_tri_dot_right(lf_r[:, rows], tri_u) for rows in chunks]
    stats, sc_b, wgk_t = {}, {}, {}
    for ci, rows in enumerate(chunks):
        bcum_c, bcum_r = bcum_cs[ci], bcum_rs[ci]
        for hd, lanes in enumerate(heads):
            bc = bcum_c[:, H_C + hd:H_C + hd + 1]
            br = bcum_r[H_C + hd:H_C + hd + 1, :]
            li_c = gz[rows, hd:hd + 1]
            li_r = gzt[hd:hd + 1, rows]
            dlog = jnp.where(causal, bc + (li_r - br), NEG)
            m_intra = jnp.max(dlog, axis=-1, keepdims=True)
            sc = qk[ci, hd] * jnp.exp(dlog - m_intra)
            bl = bc[CHUNK - 1:CHUNK, :]
            g_c = bl - bc + li_c
            g_max = jnp.max(g_c, axis=0, keepdims=True)
            wg = jnp.exp(g_c - g_max)
            kn = jnp.sum(_round_bf(wg) * kb[rows, lanes].astype(F32), axis=0, keepdims=True)
            stats[ci, hd] = (bc, m_intra, jnp.sum(sc, axis=-1, keepdims=True), bl, g_max, kn)
            sc_b[ci, hd] = sc.astype(BF)
            wgk_t[ci, hd] = (wg * kf[rows, lanes]).T.astype(BF)
    for ci, rows in enumerate(chunks):
        for hd, lanes in enumerate(heads):
            hc_scr[rows, lanes] = _dot(sc_b[ci, hd], vb[rows, lanes])
            kv_scr[ci, hd] = _dot(wgk_t[ci, hd], vb[rows, lanes])

    for ci, rows in enumerate(chunks):
        for hd, lanes in enumerate(heads):
            bc, m_intra, den_intra, bl, g_max, kn = stats[ci, hd]
            m_prev = m_scr[hd:hd + 1, 0:1]
            c_old = c_scr[hd]
            n_old = n_scr[hd:hd + 1, :]
            inter = bc + m_prev
            m_t = jnp.maximum(inter, m_intra)
            r_intra = jnp.exp(m_intra - m_t)
            w_inter = jnp.exp(inter - m_t)
            qh = q[rows, lanes]
            num = r_intra * hc_scr[rows, lanes] + w_inter * _dot(qh, c_old.astype(BF))
            qn = jnp.sum(qh.astype(F32) * _round_bf(n_old), axis=-1, keepdims=True)
            den = r_intra * den_intra + w_inter * qn
            hc_scr[rows, lanes] = num / jnp.maximum(jnp.abs(den), jnp.exp(-m_t))
            m_new = jnp.maximum(bl + m_prev, g_max)
            a = jnp.exp(bl + m_prev - m_new)
            r_state = jnp.exp(g_max - m_new)
            c_scr[hd] = a * c_old + r_state * kv_scr[ci, hd]
            n_scr[hd:hd + 1, :] = a * n_old + r_state * kn
            m_scr[hd:hd + 1, :] = jnp.broadcast_to(m_new, (1, LANES))

    mix_ref[:, 0:C_W] = (og * hc_scr[...]).astype(BF)

    n_rows = tm + top
    e_ref[top:n_rows, :] = p
    lo = SUBLANES
    s_a[lo:n_rows, :] = e_ref[lo:n_rows, :] + e_ref[lo - 1:n_rows - 1, :]
    s_b[lo:n_rows, :] = s_a[lo:n_rows, :] + s_a[lo - 2:n_rows - 2, :]
    w2 = s_a[top:n_rows, 0:POOL_G]
    s_a[lo:n_rows, :] = s_b[lo:n_rows, :] + s_b[lo - 4:n_rows - 4, :]
    w4 = s_b[top:n_rows, POOL_G:2 * POOL_G]
    w8 = s_a[top:n_rows, 2 * POOL_G:3 * POOL_G]
    w16 = s_a[top:n_rows, 3 * POOL_G:4 * POOL_G] + s_a[top - 8:n_rows - 8, 3 * POOL_G:4 * POOL_G]
    pos1 = (t * tm + lax.broadcasted_iota(jnp.int32, (tm, 1), 0) + 1).astype(F32)
    for gi, (win, wsum) in enumerate(zip(POOL_WINDOWS, (w2, w4, w8, w16))):
        lanes = slice(gi * POOL_G, (gi + 1) * POOL_G)
        mean = wsum / jnp.minimum(float(win), pos1)
        y = _dot((mean - p[:, lanes]).astype(BF), wp_ref[gi]) * ps_ref[:, lanes]
        mix_ref[:, C_W + gi * POOL_G:C_W + (gi + 1) * POOL_G] = y.astype(BF)
    pool_out[...] = e_ref[n_rows - 2 * SUBLANES:n_rows, :]
    e_ref[SUBLANES:top, :] = e_ref[n_rows - 2 * SUBLANES:n_rows, :]

    c_out[...] = c_scr[...]
    n_out[...] = n_scr[...]
    m_out[...] = m_scr[...]


def _odd_prompt(x, g, w_main, wg, wgt, bg, bgt, wp, ps, *, tm):
    b, s, _ = x.shape
    nt = s // tm
    state = lambda *shape: pl.BlockSpec((None,) + shape, lambda i, t: (i,) + (0,) * len(shape))
    return pl.pallas_call(
        functools.partial(_odd_kernel, tm=tm),
        grid=(b, nt),
        in_specs=[pl.BlockSpec((None, tm, D_MODEL), lambda i, t: (i, t, 0))]
                 + [_const_spec(a.shape) for a in (g, w_main, wg, wgt, bg, bgt, wp, ps)],
        out_specs=(pl.BlockSpec((None, tm, C_W + P_W), lambda i, t: (i, t, 0)),
                   state(H_C, DK_C, DK_C), state(SUBLANES, DK_C), state(SUBLANES, LANES),
                   state(2 * SUBLANES, P_W)),
        out_shape=(jax.ShapeDtypeStruct((b, s, C_W + P_W), BF),
                   jax.ShapeDtypeStruct((b, H_C, DK_C, DK_C), F32),
                   jax.ShapeDtypeStruct((b, SUBLANES, DK_C), F32),
                   jax.ShapeDtypeStruct((b, SUBLANES, LANES), F32),
                   jax.ShapeDtypeStruct((b, 2 * SUBLANES, P_W), F32)),
        scratch_shapes=[pltpu.VMEM((H_C, DK_C, DK_C), F32), pltpu.VMEM((SUBLANES, DK_C), F32),
                        pltpu.VMEM((SUBLANES, LANES), F32),
                        pltpu.VMEM((tm + 3 * SUBLANES, P_W), F32),
                        pltpu.VMEM((tm + 3 * SUBLANES, P_W), F32),
                        pltpu.VMEM((tm + 3 * SUBLANES, P_W), F32),
                        pltpu.VMEM((tm, C_W), F32),
                        pltpu.VMEM((tm // CHUNK, H_C, DK_C, DK_C), F32)],
        compiler_params=_params(("arbitrary", "arbitrary")),
        name="odd_prompt",
    )(x, g, w_main, wg, wgt, bg, bgt, wp, ps)


def _even_sample_kernel(x_ref, g_ref, w_ref, wf_ref, bf_ref, cw_ref, st_ref, seg_ref,
                        a_ref, q_ref, k_ref, v_ref, lf_ref, snew_ref, cn_ref):
    h = _rms(x_ref[...], g_ref[...]).astype(BF)
    u = _dot(h, w_ref[:, 0:A_W])
    gb = _dot(h, w_ref[:, A_W:2 * A_W])
    gc = _dot(h, w_ref[:, 2 * A_W:3 * A_W])
    cu = gc * u
    cw = cw_ref[...]
    prev1 = st_ref[:, A_W:2 * A_W]
    a_ref[...] = (gb * (cw[0:1] * st_ref[:, 0:A_W] + cw[1:2] * prev1 + cw[2:3] * cu)).astype(BF)
    cn_ref[:, 0:A_W] = prev1
    cn_ref[:, A_W:2 * A_W] = cu
    base = 3 * A_W
    qs = _round_bf(_dot(h, w_ref[:, base:base + B_W]) * (HD_B ** -0.5))
    k = _dot(h, w_ref[:, base + B_W:base + 2 * B_W])
    v = _dot(h, w_ref[:, base + 2 * B_W:base + 3 * B_W])
    k_ref[...] = k
    v_ref[...] = v
    lf_ref[...] = _log_sigmoid(_dot(h, wf_ref[...]) + bf_ref[...])[:, 0:H_B]
    snew_ref[...] = _tri_dot_right(qs * _round_bf(k), seg_ref[...])
    q_ref[...] = qs


def _even_sample(x, g, w_main, wf, bfl, cw, st, seg):
    n = x.shape[0]
    return pl.pallas_call(
        _even_sample_kernel,
        out_shape=(jax.ShapeDtypeStruct((n, A_W), BF),
                   jax.ShapeDtypeStruct((n, B_W), F32),
                   jax.ShapeDtypeStruct((n, B_W), F32),
                   jax.ShapeDtypeStruct((n, B_W), F32),
                   jax.ShapeDtypeStruct((n, H_B), F32),
                   jax.ShapeDtypeStruct((n, LANES), F32),
                   jax.ShapeDtypeStruct((n, 2 * A_W), F32)),
        compiler_params=pltpu.CompilerParams(vmem_limit_bytes=VMEM_LIMIT),
        name="even_sample",
    )(x, g, w_main, wf, bfl, cw, st, seg)


def _paged_kernel(pt_ref, qbd_ref, lfn_ref, snew_ref, vnew_ref, *refs, n_steps):
    npg = PAGES_PER_STEP
    k_refs = refs[0:npg]
    v_refs = refs[npg:2 * npg]
    f_refs = refs[2 * npg:3 * npg]
    o_ref = refs[3 * npg]
    m_scr, l_scr, acc_scr, suf_scr = refs[3 * npg + 1:]
    c = pl.program_id(1)
    own = (lax.broadcasted_iota(jnp.int32, (H_B, B_W), 1) // HD_B
           == lax.broadcasted_iota(jnp.int32, (H_B, B_W), 0))

    @pl.when(c == 0)
    def _():
        m_scr[...] = jnp.broadcast_to(snew_ref[...], m_scr.shape)
        l_scr[...] = jnp.ones_like(l_scr)
        acc_scr[...] = jnp.where(own, jnp.broadcast_to(_round_bf(vnew_ref[...]), (H_B, B_W)), 0.0)
        suf_scr[...] = jnp.broadcast_to(lfn_ref[...], suf_scr.shape)

    r_i = lax.broadcasted_iota(jnp.int32, (PAGE_SIZE, 2 * PAGE_SIZE), 0)
    c_i = lax.broadcasted_iota(jnp.int32, (PAGE_SIZE, 2 * PAGE_SIZE), 1)
    later_or_total = jnp.where((r_i > c_i) | (c_i >= PAGE_SIZE), 1.0, 0.0).astype(BF)
    lf_all = jnp.concatenate([f_refs[j][...] for j in range(npg)], axis=0)
    sums = _tri_dot_right(lf_all, later_or_total)

    q = qbd_ref[...]
    suf = suf_scr[:, 0:1]
    scores = []
    for j in reversed(range(npg)):
        rows = slice(j * H_B, (j + 1) * H_B)
        scores.append(_dot(q, k_refs[j][...].astype(BF)) + (sums[rows, 0:PAGE_SIZE] + suf))
        suf = suf + sums[rows, PAGE_SIZE:PAGE_SIZE + 1]
    suf_scr[...] = jnp.broadcast_to(suf, suf_scr.shape)
    m_old = m_scr[:, 0:1]
    m_new = m_old
    for s in scores:
        m_new = jnp.maximum(m_new, jnp.max(s, axis=-1, keepdims=True))
    alpha = jnp.exp(m_old - m_new)
    l = alpha * l_scr[:, 0:1]
    acc_scr[...] = alpha * acc_scr[...]
    for s, j in zip(scores, reversed(range(npg))):
        p = jnp.exp(s - m_new)
        l = l + jnp.sum(p, axis=-1, keepdims=True)
        acc_scr[...] += _dot_nt(p.astype(BF), v_refs[j][...].astype(BF))
    m_scr[...] = jnp.broadcast_to(m_new, m_scr.shape)
    l_scr[...] = jnp.broadcast_to(l, l_scr.shape)

    @pl.when(c == n_steps - 1)
    def _():
        o_ref[...] = jnp.sum(jnp.where(own, acc_scr[...] / l, 0.0), axis=0, keepdims=True)


def _paged_attention(page_table, qbd, lfn, snew, vnew, cache_k, cache_v, cache_lft):
    n, n_pages = page_table.shape
    npg = PAGES_PER_STEP
    n_steps = n_pages // npg

    def page_of(i, c, pt, j):
        return pt[i, (n_steps - 1 - c) * npg + j]

    per_sample = lambda *shape: pl.BlockSpec((None,) + shape, lambda i, c, pt: (i,) + (0,) * len(shape))
    kv_spec = lambda j: pl.BlockSpec((None, B_W, PAGE_SIZE), lambda i, c, pt: (page_of(i, c, pt, j), 0, 0))
    lf_spec = lambda j: pl.BlockSpec((None, H_B, PAGE_SIZE), lambda i, c, pt: (page_of(i, c, pt, j), 0, 0))
    in_specs = ([per_sample(H_B, B_W), per_sample(H_B, 1), per_sample(H_B, 1), per_sample(1, B_W)]
                + [kv_spec(j) for j in range(npg)] + [kv_spec(j) for j in range(npg)]
                + [lf_spec(j) for j in range(npg)])
    grid_spec = pltpu.PrefetchScalarGridSpec(
        num_scalar_prefetch=1, grid=(n, n_steps), in_specs=in_specs,
        out_specs=per_sample(1, B_W),
        scratch_shapes=[pltpu.VMEM((H_B, LANES), F32), pltpu.VMEM((H_B, LANES), F32),
                        pltpu.VMEM((H_B, B_W), F32), pltpu.VMEM((H_B, LANES), F32)])
    return pl.pallas_call(
        functools.partial(_paged_kernel, n_steps=n_steps),
        grid_spec=grid_spec,
        out_shape=jax.ShapeDtypeStruct((n, 1, B_W), F32),
        compiler_params=_params(("arbitrary", "arbitrary")),
        name="paged_attention",
    )(page_table, qbd, lfn, snew, vnew, *([cache_k] * npg), *([cache_v] * npg), *([cache_lft] * npg))


def _odd_sample_kernel(x_ref, g_ref, w_ref, wg_ref, bg_ref, wp_ref, ps_ref, c_ref, n_ref, m_ref, pool_ref,
                       mix_ref, c_out, n_out, m_out, pool_out, *, pool_div):
    nb = x_ref.shape[0]
    h = _rms(x_ref[...], g_ref[...]).astype(BF)
    q = _round_bf(_dot(h, w_ref[:, 0:C_W]))
    kf = _dot(h, w_ref[:, C_W:2 * C_W]) * (DK_C ** -0.5)
    k = _round_bf(kf)
    v = _round_bf(_dot(h, w_ref[:, 2 * C_W:3 * C_W]))
    og = jax.nn.sigmoid(_dot(h, w_ref[:, 3 * C_W:4 * C_W]))
    p = _dot(h, w_ref[:, 4 * C_W:4 * C_W + P_W])
    gz = _dot(h, wg_ref[...]) + bg_ref[...]
    lf = _log_sigmoid(gz)
    eye = (lax.broadcasted_iota(jnp.int32, (DK_C, DK_C), 0)
           == lax.broadcasted_iota(jnp.int32, (DK_C, DK_C), 1))

    for hd in range(H_C):
        lanes = slice(hd * DK_C, (hd + 1) * DK_C)
        li = gz[:, hd:hd + 1]
        lfh = lf[:, H_C + hd:H_C + hd + 1]
        m_prev = m_ref[:, hd:hd + 1]
        qh = q[:, lanes]
        kh = k[:, lanes]
        vh = v[:, lanes]
        n_old = n_ref[:, lanes]
        m_t = jnp.maximum(lfh + m_prev, li)
        sc = jnp.sum(qh * kh, axis=-1, keepdims=True) * jnp.exp(li - m_t)
        w_inter = jnp.exp(lfh + m_prev - m_t)
        qc = jnp.concatenate(
            [_dot(qh[i:i + 1, :].astype(BF), c_ref[i, hd].astype(BF)) for i in range(nb)], axis=0)
        num = _round_bf(sc) * vh + w_inter * qc
        den = sc + w_inter * jnp.sum(qh * _round_bf(n_old), axis=-1, keepdims=True)
        hout = num / jnp.maximum(jnp.abs(den), jnp.exp(-m_t))
        mix_ref[:, lanes] = (og[:, lanes] * hout).astype(BF)
        a = w_inter
        wg = jnp.exp(li - m_t)
        wgk = wg * kf[:, lanes]
        n_out[:, lanes] = a * n_old + _round_bf(wg) * kh
        m_out[:, hd:hd + 1] = m_t
        for i in range(nb):
            kd = jnp.where(eye, jnp.broadcast_to(wgk[i:i + 1, :], (DK_C, DK_C)), 0.0).astype(BF)
            vrep = jnp.broadcast_to(vh[i:i + 1, :], (DK_C, DK_C)).astype(BF)
            c_out[i, hd] = a[i:i + 1, :] * c_ref[i, hd] + _dot(kd, vrep)

    prev = pool_ref[...]
    row = lax.broadcasted_iota(jnp.int32, prev.shape, 1)
    for gi, win in enumerate(POOL_WINDOWS):
        lanes = slice(gi * POOL_G, (gi + 1) * POOL_G)
        tail = jnp.sum(jnp.where(row >= POOL_PREV - (win - 1), prev, 0.0)[:, :, lanes], axis=1)
        mean = (tail + p[:, lanes]) / pool_div[gi]
        y = _dot((mean - p[:, lanes]).astype(BF), wp_ref[gi]) * ps_ref[:, lanes]
        mix_ref[:, C_W + gi * POOL_G:C_W + (gi + 1) * POOL_G] = y.astype(BF)
    pool_out[:, 0:POOL_PREV - 1, :] = pool_ref[:, 1:POOL_PREV, :]
    for i in range(nb):
        pool_out[i, POOL_PREV - 1:POOL_PREV, :] = p[i:i + 1, :]


def _odd_sample(x, g, w_main, wg, bg, wp, ps, c, n, m, pool, *, pool_div, nb):
    nsmp = x.shape[0]
    rows = lambda w: pl.BlockSpec((nb, w), lambda i: (i, 0))
    return pl.pallas_call(
        functools.partial(_odd_sample_kernel, pool_div=pool_div),
        grid=(nsmp // nb,),
        in_specs=[rows(D_MODEL)] + [_const_spec(a.shape) for a in (g, w_main, wg, bg, wp, ps)]
                 + [pl.BlockSpec((nb, H_C, DK_C, DK_C), lambda i: (i, 0, 0, 0)),
                    rows(C_W), rows(H_C),
                    pl.BlockSpec((nb, POOL_PREV, P_W), lambda i: (i, 0, 0))],
        out_specs=(rows(C_W + P_W),
                   pl.BlockSpec((nb, H_C, DK_C, DK_C), lambda i: (i, 0, 0, 0)),
                   rows(C_W), rows(H_C),
                   pl.BlockSpec((nb, POOL_PREV, P_W), lambda i: (i, 0, 0))),
        out_shape=(jax.ShapeDtypeStruct((nsmp, C_W + P_W), BF),
                   jax.ShapeDtypeStruct((nsmp, H_C, DK_C, DK_C), F32),
                   jax.ShapeDtypeStruct((nsmp, C_W), F32),
                   jax.ShapeDtypeStruct((nsmp, H_C), F32),
                   jax.ShapeDtypeStruct((nsmp, POOL_PREV, P_W), F32)),
        compiler_params=_params(("arbitrary",)),
        name="odd_sample",
    )(x, g, w_main, wg, bg, wp, ps, c, n, m, pool)


def _pad_cols(w, n):
    return jnp.pad(w, ((0, 0), (0, n - w.shape[1])))


def kernel(x_prompt, x_sample, cache_k, cache_v, cache_logf, state_conv_a, state_mlstm_c, state_mlstm_n,
           state_mlstm_m, state_pool, state_ffn_conv, page_table, norm_mix, norm_ffn, norm_final, w_in_even,
           b_forget_even, conv_a, w_out_even, w_in_odd, b_igate_odd, b_fgate_odd, w_pool_odd, pool_scale_odd,
           w_out_odd, w_up, ffn_conv_w, ffn_conv_b, w_down):
    b, s, _ = x_prompt.shape
    nsmp = x_sample.shape[0]
    n_pages = page_table.shape[1]
    assert x_sample.shape[1] == 1 and norm_mix.shape[0] == 2
    tm = min(512, s)
    tq = min(256, s)
    assert s % tm == 0 and n_pages % PAGES_PER_STEP == 0

    xs = x_sample.reshape(nsmp, D_MODEL)
    row = lambda v: v.reshape(1, -1)
    g_final = row(norm_final)

    w = w_in_even[0]
    n_main = 3 * A_W + 3 * B_W
    w_main = w[:, :n_main].astype(BF)
    wf = _pad_cols(w[:, n_main:], LANES).astype(BF)
    wft = jnp.pad(w[:, n_main:].T, ((0, 2 * SUBLANES - H_B), (0, 0))).astype(BF)
    bfl = _pad_cols(row(b_forget_even[0]), LANES)
    bft = jnp.pad(b_forget_even[0].reshape(H_B, 1), ((0, 2 * SUBLANES - H_B), (0, 0)))
    g_mix = row(norm_mix[0])
    wo = w_out_even[0].astype(BF)
    wu_all = w_up.astype(BF)
    wd_all = w_down.astype(BF)
    ffn0 = (wu_all, ffn_conv_w[0], ffn_conv_b[0].reshape(1, D_FF), wd_all)

    wqt = w[:, 3 * A_W:3 * A_W + B_W].T.astype(BF)
    wvt = w[:, 3 * A_W + 2 * B_W:n_main].T.astype(BF)
    a_p, qt_p, k_p, v_p, kb_p, vt_p, lf_p, fc_p, fr_p, ca_p = _even_proj(
        x_prompt, g_mix, w_main, wqt, wvt, wf, wft, bfl, bft, conv_a[0], tm=tm, tk=tq)
    att_p = _fox_prompt(qt_p, kb_p, vt_p, fc_p, fr_p, tq=tq)
    xp, ff0_p = _mix_ffn(x_prompt, a_p, att_p, 0, wo, row(norm_ffn[0]), *ffn0, g_final,
                         layer=0, tm=tm, final_norm=False)

    seg = jnp.asarray((np.arange(B_W)[:, None] // HD_B == np.arange(LANES)[None, :]).astype(np.float32), dtype=BF)
    a_s, q_s, k_s, v_s, lf_s, snew_s, ca_s = _even_sample(
        xs, g_mix, w_main, wf, bfl, conv_a[0], state_conv_a[0].reshape(nsmp, 2 * A_W), seg)
    head_cols = jnp.asarray(np.arange(B_W)[None, :] // HD_B == np.arange(H_B)[:, None])
    qbd = jnp.where(head_cols[None], q_s[:, None, :], 0.0).astype(BF)
    n_phys = cache_k.shape[1]
    kt = cache_k[0].transpose(0, 2, 3, 1).reshape(n_phys, B_W, PAGE_SIZE)
    vt = cache_v[0].transpose(0, 2, 3, 1).reshape(n_phys, B_W, PAGE_SIZE)
    att_s = _paged_attention(
        page_table, qbd, lf_s.reshape(nsmp, H_B, 1), snew_s[:, :H_B].reshape(nsmp, H_B, 1),
        v_s.reshape(nsmp, 1, B_W), kt, vt, cache_logf[0].transpose(0, 2, 1))
    xs1, ff0_s = _mix_ffn_sample(xs, a_s, att_s.reshape(nsmp, B_W).astype(BF), wo, row(norm_ffn[0]), *ffn0,
                                 g_final, state_ffn_conv[0].reshape(nsmp, 2 * D_FF), layer=0, final_norm=False)

    w = w_in_odd[0]
    n_qkvo = 4 * C_W
    w_main = jnp.concatenate([w[:, :n_qkvo], w[:, n_qkvo + 2 * H_C:]], axis=1).astype(BF)
    w_gate = w[:, n_qkvo:n_qkvo + 2 * H_C]
    wg = _pad_cols(w_gate, LANES).astype(BF)
    wgt = jnp.pad(w_gate.T, ((0, 2 * SUBLANES - 2 * H_C), (0, 0))).astype(BF)
    b_gate = jnp.concatenate([b_igate_odd[0], b_fgate_odd[0]])
    bg = _pad_cols(row(b_gate), LANES)
    bgt = jnp.pad(b_gate.reshape(2 * H_C, 1), ((0, 2 * SUBLANES - 2 * H_C), (0, 0)))
    g_mix = row(norm_mix[1])
    wp = w_pool_odd[0].astype(BF)
    ps = row(pool_scale_odd[0])
    wo = w_out_odd[0].astype(BF)
    ffn1 = (wu_all, ffn_conv_w[1], ffn_conv_b[1].reshape(1, D_FF), wd_all)

    mix_p, c_p, n_p, m_p, pool_p = _odd_prompt(xp, g_mix, w_main, wg, wgt, bg, bgt, wp, ps, tm=tm)
    y_p, ff1_p = _mix_ffn(xp, mix_p, mix_p, 1, wo, row(norm_ffn[1]), *ffn1, g_final,
                          layer=1, tm=tm, final_norm=True)

    pos0 = n_pages * PAGE_SIZE
    pool_div = tuple(float(min(win, pos0 + 1)) for win in POOL_WINDOWS)
    mix_s, c_s, n_s, m_s, pool_s = _odd_sample(
        xs1, g_mix, w_main, wg, bg, wp, ps, state_mlstm_c[0], state_mlstm_n[0].reshape(nsmp, C_W),
        state_mlstm_m[0], state_pool[0], pool_div=pool_div, nb=min(8, nsmp))
    y_s, ff1_s = _mix_ffn_sample(xs1, mix_s[:, :C_W], mix_s[:, C_W:], wo, row(norm_ffn[1]), *ffn1, g_final,
                                 state_ffn_conv[1].reshape(nsmp, 2 * D_FF), layer=1, final_norm=True)

    heads = lambda z: z.reshape(z.shape[:-1] + (H_B, HD_B))
    return (
        y_p, y_s.reshape(nsmp, 1, D_MODEL),
        heads(k_p)[None], heads(k_s).reshape(1, nsmp, 1, H_B, HD_B),
        heads(v_p)[None], heads(v_s).reshape(1, nsmp, 1, H_B, HD_B),
        lf_p[None], lf_s.reshape(1, nsmp, 1, H_B),
        ca_p[None], ca_s.reshape(1, nsmp, 2, A_W),
        c_p[None], c_s[None],
        n_p[:, :H_C][None], n_s.reshape(1, nsmp, H_C, DK_C),
        m_p[:, :H_C, 0][None], m_s[None],
        pool_p[:, 1:][None], pool_s[None],
        jnp.stack([ff0_p, ff1_p]),
        jnp.stack([ff0_s.reshape(nsmp, 2, D_FF), ff1_s.reshape(nsmp, 2, D_FF)]),
    )
```

```python
import functools

import numpy as np
import jax
import jax.numpy as jnp
from jax import lax
from jax.experimental import pallas as pl
from jax.experimental.pallas import tpu as pltpu

D_MODEL = 1024
A_W = 512
B_W = 512
H_B = 8
HD_B = 64
C_W = 512
H_C = 4
DK_C = 128
P_W = 512
POOL_WINDOWS = (2, 4, 8, 16)
POOL_G = 128
POOL_PREV = 15
D_FF = 2816
PAGE_SIZE = 128
CHUNK = 128
EPS = 1e-6

LANES = 128
SUBLANES = 8
FF_CHUNK = 256
N_FF = D_FF // FF_CHUNK
PAGES_PER_STEP = 16
VMEM_LIMIT = 56 * 1024 * 1024
NEG = -1e30

BF = jnp.bfloat16
F32 = jnp.float32


def _rms(x, g):
    return x * lax.rsqrt(jnp.mean(x * x, axis=-1, keepdims=True) + EPS) * g


def _dot(a, b):
    return jnp.dot(a, b, preferred_element_type=F32)


def _dot_nt(a, b):
    return lax.dot_general(a, b, (((1,), (1,)), ((), ())), preferred_element_type=F32)


def _split3(x):
    hi = x.astype(BF)
    r = x - hi.astype(F32)
    mid = r.astype(BF)
    lo = (r - mid.astype(F32)).astype(BF)
    return hi, mid, lo


def _tri_dot_left(tri, x):
    hi, mid, lo = _split3(x)
    return _dot(tri, hi) + _dot(tri, mid) + _dot(tri, lo)


def _tri_dot_right(x, tri):
    hi, mid, lo = _split3(x)
    return _dot(hi, tri) + _dot(mid, tri) + _dot(lo, tri)


def _log_sigmoid(x):
    return jnp.minimum(x, 0.0) - jnp.log1p(jnp.exp(-jnp.abs(x)))


def _tri(n, lower):
    r = lax.broadcasted_iota(jnp.int32, (n, n), 0)
    c = lax.broadcasted_iota(jnp.int32, (n, n), 1)
    keep = (c <= r) if lower else (r <= c)
    return jnp.where(keep, 1.0, 0.0).astype(BF)


def _round_bf(x):
    return x.astype(BF).astype(F32)


def _const_spec(shape):
    zeros = (0,) * len(shape)
    return pl.BlockSpec(shape, lambda *_: zeros, pipeline_mode=pl.Buffered(1))


def _layer_spec(shape, layer):
    zeros = (0,) * (len(shape) - 1)
    return pl.BlockSpec((None,) + tuple(shape[1:]), lambda *_: (layer,) + zeros, pipeline_mode=pl.Buffered(1))


def _params(sem):
    return pltpu.CompilerParams(dimension_semantics=sem, vmem_limit_bytes=VMEM_LIMIT)


def _even_proj_kernel(x_ref, g_ref, w_ref, wqt_ref, wvt_ref, wf_ref, wft_ref, bf_ref, bft_ref, cw_ref,
                      a_ref, qt_ref, k_ref, v_ref, kb_ref, vt_ref, lf_ref, fc_ref, fr_ref, cn_ref,
                      ext_ref, ccol_ref, crow_ref, *, tm, tk):
    t = pl.program_id(1)

    @pl.when(t == 0)
    def _():
        ext_ref[0:SUBLANES, :] = jnp.zeros((SUBLANES, A_W), F32)
        ccol_ref[...] = jnp.zeros_like(ccol_ref)
        crow_ref[...] = jnp.zeros_like(crow_ref)

    h = _rms(x_ref[...], g_ref[...]).astype(BF)
    lf = _log_sigmoid(_dot(h, wf_ref[...]) + bf_ref[...])
    lft = _log_sigmoid(_dot_nt(wft_ref[...], h) + bft_ref[...])
    lf_ref[...] = lf[:, 0:H_B]

    u = _dot(h, w_ref[:, 0:A_W])
    gb = _dot(h, w_ref[:, A_W:2 * A_W])
    gc = _dot(h, w_ref[:, 2 * A_W:3 * A_W])

    cs = _tri_dot_left(_tri(tm, True), lf) + ccol_ref[0:1, :]
    fc_ref[...] = cs[:, 0:H_B]
    ccol_ref[...] = jnp.broadcast_to(cs[tm - 1:tm, :], ccol_ref.shape)
    cst = _tri_dot_right(lft, _tri(tm, False)) + crow_ref[:, 0:1]
    for j in range(tm // tk):
        fr_ref[j] = cst[0:H_B, j * tk:(j + 1) * tk]
    crow_ref[...] = jnp.broadcast_to(cst[:, tm - 1:tm], crow_ref.shape)

    cu = gc * u
    ext_ref[SUBLANES:SUBLANES + tm, :] = cu
    cw = cw_ref[...]
    conv = (cw[0:1] * ext_ref[SUBLANES - 2:SUBLANES - 2 + tm, :]
            + cw[1:2] * ext_ref[SUBLANES - 1:SUBLANES - 1 + tm, :]
            + cw[2:3] * cu)
    a_ref[...] = (gb * conv).astype(BF)
    cn_ref[...] = ext_ref[tm + SUBLANES - 2:tm + SUBLANES, :]
    ext_ref[0:SUBLANES, :] = ext_ref[tm:tm + SUBLANES, :]

    base = 3 * A_W
    k = _dot(h, w_ref[:, base + B_W:base + 2 * B_W])
    k_ref[...] = k
    kb_ref[...] = k.astype(BF)
    v_ref[...] = _dot(h, w_ref[:, base + 2 * B_W:base + 3 * B_W])
    qt = (_dot_nt(wqt_ref[...], h) * (HD_B ** -0.5)).astype(BF)
    vt = _dot_nt(wvt_ref[...], h).astype(BF)
    for j in range(tm // tk):
        qt_ref[j] = qt[:, j * tk:(j + 1) * tk]
        vt_ref[j] = vt[:, j * tk:(j + 1) * tk]


def _even_proj(x, g, w_main, wqt, wvt, wf, wft, bfl, bft, cw, *, tm, tk):
    b, s, _ = x.shape
    nt = s // tm
    row = lambda w: pl.BlockSpec((None, tm, w), lambda i, t: (i, t, 0))
    chunked = lambda rows: pl.BlockSpec((None, tm // tk, rows, tk), lambda i, t: (i, t, 0, 0))
    out_shape = (
        jax.ShapeDtypeStruct((b, s, A_W), BF),
        jax.ShapeDtypeStruct((b, s // tk, B_W, tk), BF),
        jax.ShapeDtypeStruct((b, s, B_W), F32),
        jax.ShapeDtypeStruct((b, s, B_W), F32),
        jax.ShapeDtypeStruct((b, s, B_W), BF),
        jax.ShapeDtypeStruct((b, s // tk, B_W, tk), BF),
        jax.ShapeDtypeStruct((b, s, H_B), F32),
        jax.ShapeDtypeStruct((b, s, H_B), F32),
        jax.ShapeDtypeStruct((b, s // tk, H_B, tk), F32),
        jax.ShapeDtypeStruct((b, 2, A_W), F32),
    )
    out_specs = (
        row(A_W), chunked(B_W), row(B_W), row(B_W), row(B_W), chunked(B_W), row(H_B), row(H_B),
        chunked(H_B),
        pl.BlockSpec((None, 2, A_W), lambda i, t: (i, 0, 0)),
    )
    in_specs = [row(D_MODEL)] + [_const_spec(a.shape) for a in (g, w_main, wqt, wvt, wf, wft, bfl, bft, cw)]
    return pl.pallas_call(
        functools.partial(_even_proj_kernel, tm=tm, tk=tk),
        grid=(b, nt), in_specs=in_specs, out_specs=out_specs, out_shape=out_shape,
        scratch_shapes=[pltpu.VMEM((tm + SUBLANES, A_W), F32),
                        pltpu.VMEM((SUBLANES, LANES), F32),
                        pltpu.VMEM((2 * SUBLANES, LANES), F32)],
        compiler_params=_params(("arbitrary", "arbitrary")),
        name="even_proj",
    )(x, g, w_main, wqt, wvt, wf, wft, bfl, bft, cw)


def _fox_kernel(qt_ref, k_ref, vt_ref, fc_ref, fr_ref, o_ref, qm_scr, m_scr, l_scr, acc_scr, *, tq):
    qi = pl.program_id(1)
    sub = lax.broadcasted_iota(jnp.int32, (LANES, tq), 0)
    for h in range(H_B):
        hp, hh = divmod(h, 2)
        qp = qt_ref[hp * LANES:(hp + 1) * LANES, :]
        in_head = (sub < HD_B) if hh == 0 else (sub >= HD_B)
        qm_scr[h] = jnp.where(in_head, qp, jnp.zeros_like(qp))
    m_scr[...] = jnp.full(m_scr.shape, NEG, F32)
    l_scr[...] = jnp.zeros_like(l_scr)
    acc_scr[...] = jnp.zeros_like(acc_scr)
    causal = (lax.broadcasted_iota(jnp.int32, (tq, tq), 0) <= lax.broadcasted_iota(jnp.int32, (tq, tq), 1))

    def step(j, masked):
        off = pl.multiple_of(j * tq, tq)
        group = lambda h: slice((h // 2) * LANES, (h // 2 + 1) * LANES)
        scores = [_dot(k_ref[pl.ds(off, tq), group(h)], qm_scr[h]) for h in range(H_B)]
        probs, alphas = [], []
        for h in range(H_B):
            s = scores[h] - fc_ref[pl.ds(off, tq), h:h + 1]
            if masked:
                s = jnp.where(causal, s, NEG)
            f_q = fr_ref[h:h + 1, :]
            m_old = m_scr[h:h + 1, :]
            m_new = jnp.maximum(m_old, jnp.max(s, axis=0, keepdims=True) + f_q)
            p = jnp.exp(s - (m_new - f_q))
            alpha = jnp.exp(m_old - m_new)
            l_scr[h:h + 1, :] = alpha * l_scr[h:h + 1, :] + jnp.sum(p, axis=0, keepdims=True)
            m_scr[h:h + 1, :] = m_new
            probs.append(p.astype(BF))
            alphas.append(alpha)
        for h in range(H_B):
            hp, hh = divmod(h, 2)
            rows = slice(hh * HD_B, (hh + 1) * HD_B)
            pv = _dot(vt_ref[j, group(h), :], probs[h])
            acc_scr[hp, rows, :] = alphas[h] * acc_scr[hp, rows, :] + pv[rows, :]

    def body(j, carry):
        step(j, False)
        return carry

    lax.fori_loop(0, qi, body, 0)
    step(qi, True)
    for hp in range(H_B // 2):
        denom = jnp.where(sub < HD_B, l_scr[2 * hp:2 * hp + 1, :], l_scr[2 * hp + 1:2 * hp + 2, :])
        o_ref[:, hp * LANES:(hp + 1) * LANES] = (acc_scr[hp] / denom).T.astype(BF)


def _fox_prompt(qt, kb, vt, fc, fr, *, tq):
    b, nq, _, _ = qt.shape
    s = nq * tq
    return pl.pallas_call(
        functools.partial(_fox_kernel, tq=tq),
        grid=(b, nq),
        in_specs=[
            pl.BlockSpec((None, None, B_W, tq), lambda i, t: (i, t, 0, 0)),
            pl.BlockSpec((None, s, B_W), lambda i, t: (i, 0, 0)),
            pl.BlockSpec((None, nq, B_W, tq), lambda i, t: (i, 0, 0, 0)),
            pl.BlockSpec((None, s, H_B), lambda i, t: (i, 0, 0)),
            pl.BlockSpec((None, None, H_B, tq), lambda i, t: (i, t, 0, 0)),
        ],
        out_specs=pl.BlockSpec((None, tq, B_W), lambda i, t: (i, t, 0)),
        out_shape=jax.ShapeDtypeStruct((b, s, B_W), BF),
        scratch_shapes=[pltpu.VMEM((H_B, LANES, tq), BF), pltpu.VMEM((H_B, tq), F32),
                        pltpu.VMEM((H_B, tq), F32), pltpu.VMEM((H_B // 2, LANES, tq), F32)],
        compiler_params=_params(("arbitrary", "arbitrary")),
        name="fox_prompt",
    )(qt, kb, vt, fc, fr)


def _mix_ffn_kernel(x_ref, ma_ref, mb_ref, wo_ref, gf_ref, wu_ref, cw_ref, cb_ref, wd_ref, gfin_ref,
                    y_ref, st_ref, acc_ref, h_ref, act_ref, ext_ref, carry_ref, *, tm, final_norm):
    t = pl.program_id(1)

    @pl.when(t == 0)
    def _():
        carry_ref[...] = jnp.zeros_like(carry_ref)

    half = wo_ref.shape[0] // 2
    x1 = x_ref[...] + _dot(ma_ref[...], wo_ref[0:half, :]) + _dot(mb_ref[...], wo_ref[half:2 * half, :])
    acc_ref[...] = x1
    h_ref[...] = _rms(x1, gf_ref[...]).astype(BF)

    def up(c):
        gate_cols = slice(c * FF_CHUNK, (c + 1) * FF_CHUNK)
        lin_cols = slice(D_FF + c * FF_CHUNK, D_FF + (c + 1) * FF_CHUNK)
        return _dot(h_ref[...], wu_ref[:, gate_cols]), _dot(h_ref[...], wu_ref[:, lin_cols])

    nxt = up(0)
    for c in range(N_FF):
        cols = slice(c * FF_CHUNK, (c + 1) * FF_CHUNK)
        g, u = nxt
        if c + 1 < N_FF:
            nxt = up(c + 1)
        ext = ext_ref.at[c % 2]
        ext[0:SUBLANES, :] = carry_ref[c]
        ext[SUBLANES:SUBLANES + tm, :] = g
        cw = cw_ref[:, cols]
        gconv = (cw[0:1] * ext[SUBLANES - 2:SUBLANES - 2 + tm, :]
                 + cw[1:2] * ext[SUBLANES - 1:SUBLANES - 1 + tm, :]
                 + cw[2:3] * g + cb_ref[:, cols])
        act_ref[:, cols] = (gconv * jax.nn.sigmoid(gconv) * u).astype(BF)
        carry_ref[c] = ext[tm:tm + SUBLANES, :]
        st_ref[:, cols] = ext[tm + SUBLANES - 2:tm + SUBLANES, :]
    y = acc_ref[...] + _dot(act_ref[...], wd_ref[...])
    y_ref[...] = _rms(y, gfin_ref[...]) if final_norm else y


def _mix_ffn(x, ma, mb, lane_blk_b, wo, gf, wu, cw, cb, wd, gfin, *, layer, tm, final_norm):
    b, s, _ = x.shape
    nt = s // tm
    half = wo.shape[0] // 2
    return pl.pallas_call(
        functools.partial(_mix_ffn_kernel, tm=tm, final_norm=final_norm),
        grid=(b, nt),
        in_specs=[
            pl.BlockSpec((None, tm, D_MODEL), lambda i, t: (i, t, 0)),
            pl.BlockSpec((None, tm, half), lambda i, t: (i, t, 0)),
            pl.BlockSpec((None, tm, half), lambda i, t: (i, t, lane_blk_b)),
            _const_spec(wo.shape), _const_spec(gf.shape), _layer_spec(wu.shape, layer),
            _const_spec(cw.shape), _const_spec(cb.shape), _layer_spec(wd.shape, layer), _const_spec(gfin.shape),
        ],
        out_specs=(pl.BlockSpec((None, tm, D_MODEL), lambda i, t: (i, t, 0)),
                   pl.BlockSpec((None, 2, D_FF), lambda i, t: (i, 0, 0))),
        out_shape=(jax.ShapeDtypeStruct((b, s, D_MODEL), F32),
                   jax.ShapeDtypeStruct((b, 2, D_FF), F32)),
        scratch_shapes=[pltpu.VMEM((tm, D_MODEL), F32), pltpu.VMEM((tm, D_MODEL), BF),
                        pltpu.VMEM((tm, D_FF), BF),
                        pltpu.VMEM((2, tm + SUBLANES, FF_CHUNK), F32),
                        pltpu.VMEM((N_FF, SUBLANES, FF_CHUNK), F32)],
        compiler_params=_params(("arbitrary", "arbitrary")),
        name="mix_ffn",
    )(x, ma, mb, wo, gf, wu, cw, cb, wd, gfin)


def _mix_ffn_sample_kernel(x_ref, ma_ref, mb_ref, wo_ref, gf_ref, wu_ref, cw_ref, cb_ref, wd_ref,
                           gfin_ref, st_ref, y_ref, stn_ref, *, final_norm):
    half = wo_ref.shape[0] // 2
    x1 = x_ref[...] + _dot(ma_ref[...], wo_ref[0:half, :]) + _dot(mb_ref[...], wo_ref[half:2 * half, :])
    h = _rms(x1, gf_ref[...]).astype(BF)
    y_ref[...] = x1
    for c in range(N_FF):
        cols = slice(c * FF_CHUNK, (c + 1) * FF_CHUNK)
        cols1 = slice(D_FF + c * FF_CHUNK, D_FF + (c + 1) * FF_CHUNK)
        g = _dot(h, wu_ref[:, cols])
        u = _dot(h, wu_ref[:, cols1])
        cw = cw_ref[:, cols]
        prev1 = st_ref[:, cols1]
        gconv = cw[0:1] * st_ref[:, cols] + cw[1:2] * prev1 + cw[2:3] * g + cb_ref[:, cols]
        act = (gconv * jax.nn.sigmoid(gconv) * u).astype(BF)
        y_ref[...] += _dot(act, wd_ref[cols, :])
        stn_ref[:, cols] = prev1
        stn_ref[:, cols1] = g
    if final_norm:
        y_ref[...] = _rms(y_ref[...], gfin_ref[...])


def _mix_ffn_sample(x, ma, mb, wo, gf, wu, cw, cb, wd, gfin, st, *, layer, final_norm):
    n = x.shape[0]
    whole = lambda a: _const_spec(a.shape)
    out_shape = (jax.ShapeDtypeStruct((n, D_MODEL), F32), jax.ShapeDtypeStruct((n, 2 * D_FF), F32))
    return pl.pallas_call(
        functools.partial(_mix_ffn_sample_kernel, final_norm=final_norm),
        grid=(1,),
        in_specs=[whole(x), whole(ma), whole(mb), whole(wo), whole(gf), _layer_spec(wu.shape, layer), whole(cw),
                  whole(cb), _layer_spec(wd.shape, layer), whole(gfin), whole(st)],
        out_specs=tuple(pl.BlockSpec(o.shape, lambda i: (0, 0)) for o in out_shape),
        out_shape=out_shape,
        compiler_params=_params(("arbitrary",)),
        name="mix_ffn_sample",
    )(x, ma, mb, wo, gf, wu, cw, cb, wd, gfin, st)


def _odd_kernel(x_ref, g_ref, w_ref, wg_ref, wgt_ref, bg_ref, bgt_ref, wp_ref, ps_ref,
                mix_ref, c_out, n_out, m_out, pool_out,
                c_scr, n_scr, m_scr, e_ref, s_a, s_b, hc_scr, kv_scr, *, tm):
    t = pl.program_id(1)
    top = 3 * SUBLANES

    @pl.when(t == 0)
    def _():
        c_scr[...] = jnp.zeros_like(c_scr)
        n_scr[...] = jnp.zeros_like(n_scr)
        m_scr[...] = jnp.zeros_like(m_scr)
        e_ref[0:top, :] = jnp.zeros((top, P_W), F32)
        s_a[0:SUBLANES, :] = jnp.zeros((SUBLANES, P_W), F32)
        s_b[0:SUBLANES, :] = jnp.zeros((SUBLANES, P_W), F32)

    h = _rms(x_ref[...], g_ref[...]).astype(BF)
    gz = _dot(h, wg_ref[...]) + bg_ref[...]
    gzt = _dot_nt(wgt_ref[...], h) + bgt_ref[...]
    lf_c = _log_sigmoid(gz)
    lf_r = _log_sigmoid(gzt)
    q = _dot(h, w_ref[:, 0:C_W]).astype(BF)
    kf = _dot(h, w_ref[:, C_W:2 * C_W]) * (DK_C ** -0.5)
    kb = kf.astype(BF)
    vb = _dot(h, w_ref[:, 2 * C_W:3 * C_W]).astype(BF)
    og = jax.nn.sigmoid(_dot(h, w_ref[:, 3 * C_W:4 * C_W]))
    p = _dot(h, w_ref[:, 4 * C_W:4 * C_W + P_W])

    tri_l = _tri(CHUNK, True)
    tri_u = _tri(CHUNK, False)
    r_i = lax.broadcasted_iota(jnp.int32, (CHUNK, CHUNK), 0)
    c_i = lax.broadcasted_iota(jnp.int32, (CHUNK, CHUNK), 1)
    causal = c_i <= r_i

    chunks = [slice(ci * CHUNK, (ci + 1) * CHUNK) for ci in range(tm // CHUNK)]
    heads = [slice(hd * DK_C, (hd + 1) * DK_C) for hd in range(H_C)]
    qk = {(ci, hd): _dot_nt(q[rows, lanes], kb[rows, lanes])
          for ci, rows in enumerate(chunks) for hd, lanes in enumerate(heads)}
    bcum_cs = [_tri_dot_left(tri_l, lf_c[rows, :]) for rows in chunks]
    bcum_rs = [_tri_dot_right(lf_r[:, rows], tri_u) for rows in chunks]
    stats, sc_b, wgk_t = {}, {}, {}
    for ci, rows in enumerate(chunks):
        bcum_c, bcum_r = bcum_cs[ci], bcum_rs[ci]
        for hd, lanes in enumerate(heads):
            bc = bcum_c[:, H_C + hd:H_C + hd + 1]
            br = bcum_r[H_C + hd:H_C + hd + 1, :]
            li_c = gz[rows, hd:hd + 1]
            li_r = gzt[hd:hd + 1, rows]
            dlog = jnp.where(causal, bc + (li_r - br), NEG)
            m_intra = jnp.max(dlog, axis=-1, keepdims=True)
            sc = qk[ci, hd] * jnp.exp(dlog - m_intra)
            bl = bc[CHUNK - 1:CHUNK, :]
            g_c = bl - bc + li_c
            g_max = jnp.max(g_c, axis=0, keepdims=True)
            wg = jnp.exp(g_c - g_max)
            kn = jnp.sum(_round_bf(wg) * kb[rows, lanes].astype(F32), axis=0, keepdims=True)
            stats[ci, hd] = (bc, m_intra, jnp.sum(sc, axis=-1, keepdims=True), bl, g_max, kn)
            sc_b[ci, hd] = sc.astype(BF)
            wgk_t[ci, hd] = (wg * kf[rows, lanes]).T.astype(BF)
    for ci, rows in enumerate(chunks):
        for hd, lanes in enumerate(heads):
            hc_scr[rows, lanes] = _dot(sc_b[ci, hd], vb[rows, lanes])
            kv_scr[ci, hd] = _dot(wgk_t[ci, hd], vb[rows, lanes])

    for ci, rows in enumerate(chunks):
        for hd, lanes in enumerate(heads):
            bc, m_intra, den_intra, bl, g_max, kn = stats[ci, hd]
            m_prev = m_scr[hd:hd + 1, 0:1]
            c_old = c_scr[hd]
            n_old = n_scr[hd:hd + 1, :]
            inter = bc + m_prev
            m_t = jnp.maximum(inter, m_intra)
            r_intra = jnp.exp(m_intra - m_t)
            w_inter = jnp.exp(inter - m_t)
            qh = q[rows, lanes]
            num = r_intra * hc_scr[rows, lanes] + w_inter * _dot(qh, c_old.astype(BF))
            qn = jnp.sum(qh.astype(F32) * _round_bf(n_old), axis=-1, keepdims=True)
            den = r_intra * den_intra + w_inter * qn
            hc_scr[rows, lanes] = num / jnp.maximum(jnp.abs(den), jnp.exp(-m_t))
            m_new = jnp.maximum(bl + m_prev, g_max)
            a = jnp.exp(bl + m_prev - m_new)
            r_state = jnp.exp(g_max - m_new)
            c_scr[hd] = a * c_old + r_state * kv_scr[ci, hd]
            n_scr[hd:hd + 1, :] = a * n_old + r_state * kn
            m_scr[hd:hd + 1, :] = jnp.broadcast_to(m_new, (1, LANES))

    mix_ref[:, 0:C_W] = (og * hc_scr[...]).astype(BF)

    n_rows = tm + top
    e_ref[top:n_rows, :] = p
    lo = SUBLANES
    s_a[lo:n_rows, :] = e_ref[lo:n_rows, :] + e_ref[lo - 1:n_rows - 1, :]
    s_b[lo:n_rows, :] = s_a[lo:n_rows, :] + s_a[lo - 2:n_rows - 2, :]
    w2 = s_a[top:n_rows, 0:POOL_G]
    s_a[lo:n_rows, :] = s_b[lo:n_rows, :] + s_b[lo - 4:n_rows - 4, :]
    w4 = s_b[top:n_rows, POOL_G:2 * POOL_G]
    w8 = s_a[top:n_rows, 2 * POOL_G:3 * POOL_G]
    w16 = s_a[top:n_rows, 3 * POOL_G:4 * POOL_G] + s_a[top - 8:n_rows - 8, 3 * POOL_G:4 * POOL_G]
    pos1 = (t * tm + lax.broadcasted_iota(jnp.int32, (tm, 1), 0) + 1).astype(F32)
    for gi, (win, wsum) in enumerate(zip(POOL_WINDOWS, (w2, w4, w8, w16))):
        lanes = slice(gi * POOL_G, (gi + 1) * POOL_G)
        mean = wsum / jnp.minimum(float(win), pos1)
        y = _dot((mean - p[:, lanes]).astype(BF), wp_ref[gi]) * ps_ref[:, lanes]
        mix_ref[:, C_W + gi * POOL_G:C_W + (gi + 1) * POOL_G] = y.astype(BF)
    pool_out[...] = e_ref[n_rows - 2 * SUBLANES:n_rows, :]
    e_ref[SUBLANES:top, :] = e_ref[n_rows - 2 * SUBLANES:n_rows, :]

    c_out[...] = c_scr[...]
    n_out[...] = n_scr[...]
    m_out[...] = m_scr[...]


def _odd_prompt(x, g, w_main, wg, wgt, bg, bgt, wp, ps, *, tm):
    b, s, _ = x.shape
    nt = s // tm
    state = lambda *shape: pl.BlockSpec((None,) + shape, lambda i, t: (i,) + (0,) * len(shape))
    return pl.pallas_call(
        functools.partial(_odd_kernel, tm=tm),
        grid=(b, nt),
        in_specs=[pl.BlockSpec((None, tm, D_MODEL), lambda i, t: (i, t, 0))]
                 + [_const_spec(a.shape) for a in (g, w_main, wg, wgt, bg, bgt, wp, ps)],
        out_specs=(pl.BlockSpec((None, tm, C_W + P_W), lambda i, t: (i, t, 0)),
                   state(H_C, DK_C, DK_C), state(SUBLANES, DK_C), state(SUBLANES, LANES),
                   state(2 * SUBLANES, P_W)),
        out_shape=(jax.ShapeDtypeStruct((b, s, C_W + P_W), BF),
                   jax.ShapeDtypeStruct((b, H_C, DK_C, DK_C), F32),
                   jax.ShapeDtypeStruct((b, SUBLANES, DK_C), F32),
                   jax.ShapeDtypeStruct((b, SUBLANES, LANES), F32),
                   jax.ShapeDtypeStruct((b, 2 * SUBLANES, P_W), F32)),
        scratch_shapes=[pltpu.VMEM((H_C, DK_C, DK_C), F32), pltpu.VMEM((SUBLANES, DK_C), F32),
                        pltpu.VMEM((SUBLANES, LANES), F32),
                        pltpu.VMEM((tm + 3 * SUBLANES, P_W), F32),
                        pltpu.VMEM((tm + 3 * SUBLANES, P_W), F32),
                        pltpu.VMEM((tm + 3 * SUBLANES, P_W), F32),
                        pltpu.VMEM((tm, C_W), F32),
                        pltpu.VMEM((tm // CHUNK, H_C, DK_C, DK_C), F32)],
        compiler_params=_params(("arbitrary", "arbitrary")),
        name="odd_prompt",
    )(x, g, w_main, wg, wgt, bg, bgt, wp, ps)


def _even_sample_kernel(x_ref, g_ref, w_ref, wf_ref, bf_ref, cw_ref, st_ref, seg_ref,
                        a_ref, q_ref, k_ref, v_ref, lf_ref, snew_ref, cn_ref):
    h = _rms(x_ref[...], g_ref[...]).astype(BF)
    u = _dot(h, w_ref[:, 0:A_W])
    gb = _dot(h, w_ref[:, A_W:2 * A_W])
    gc = _dot(h, w_ref[:, 2 * A_W:3 * A_W])
    cu = gc * u
    cw = cw_ref[...]
    prev1 = st_ref[:, A_W:2 * A_W]
    a_ref[...] = (gb * (cw[0:1] * st_ref[:, 0:A_W] + cw[1:2] * prev1 + cw[2:3] * cu)).astype(BF)
    cn_ref[:, 0:A_W] = prev1
    cn_ref[:, A_W:2 * A_W] = cu
    base = 3 * A_W
    qs = _round_bf(_dot(h, w_ref[:, base:base + B_W]) * (HD_B ** -0.5))
    k = _dot(h, w_ref[:, base + B_W:base + 2 * B_W])
    v = _dot(h, w_ref[:, base + 2 * B_W:base + 3 * B_W])
    k_ref[...] = k
    v_ref[...] = v
    lf_ref[...] = _log_sigmoid(_dot(h, wf_ref[...]) + bf_ref[...])[:, 0:H_B]
    snew_ref[...] = _tri_dot_right(qs * _round_bf(k), seg_ref[...])
    q_ref[...] = qs


def _even_sample(x, g, w_main, wf, bfl, cw, st, seg):
    n = x.shape[0]
    return pl.pallas_call(
        _even_sample_kernel,
        out_shape=(jax.ShapeDtypeStruct((n, A_W), BF),
                   jax.ShapeDtypeStruct((n, B_W), F32),
                   jax.ShapeDtypeStruct((n, B_W), F32),
                   jax.ShapeDtypeStruct((n, B_W), F32),
                   jax.ShapeDtypeStruct((n, H_B), F32),
                   jax.ShapeDtypeStruct((n, LANES), F32),
                   jax.ShapeDtypeStruct((n, 2 * A_W), F32)),
        compiler_params=pltpu.CompilerParams(vmem_limit_bytes=VMEM_LIMIT),
        name="even_sample",
    )(x, g, w_main, wf, bfl, cw, st, seg)


def _paged_kernel(pt_ref, qbd_ref, lfn_ref, snew_ref, vnew_ref, *refs, n_steps):
    npg = PAGES_PER_STEP
    k_refs = refs[0:npg]
    v_refs = refs[npg:2 * npg]
    f_refs = refs[2 * npg:3 * npg]
    o_ref = refs[3 * npg]
    m_scr, l_scr, acc_scr, suf_scr = refs[3 * npg + 1:]
    c = pl.program_id(1)
    own = (lax.broadcasted_iota(jnp.int32, (H_B, B_W), 1) // HD_B
           == lax.broadcasted_iota(jnp.int32, (H_B, B_W), 0))

    @pl.when(c == 0)
    def _():
        m_scr[...] = jnp.broadcast_to(snew_ref[...], m_scr.shape)
        l_scr[...] = jnp.ones_like(l_scr)
        acc_scr[...] = jnp.where(own, jnp.broadcast_to(_round_bf(vnew_ref[...]), (H_B, B_W)), 0.0)
        suf_scr[...] = jnp.broadcast_to(lfn_ref[...], suf_scr.shape)

    r_i = lax.broadcasted_iota(jnp.int32, (PAGE_SIZE, 2 * PAGE_SIZE), 0)
    c_i = lax.broadcasted_iota(jnp.int32, (PAGE_SIZE, 2 * PAGE_SIZE), 1)
    later_or_total = jnp.where((r_i > c_i) | (c_i >= PAGE_SIZE), 1.0, 0.0).astype(BF)
    lf_all = jnp.concatenate([f_refs[j][...] for j in range(npg)], axis=0)
    sums = _tri_dot_right(lf_all, later_or_total)

    q = qbd_ref[...]
    suf = suf_scr[:, 0:1]
    scores = []
    for j in reversed(range(npg)):
        rows = slice(j * H_B, (j + 1) * H_B)
        scores.append(_dot(q, k_refs[j][...].astype(BF)) + (sums[rows, 0:PAGE_SIZE] + suf))
        suf = suf + sums[rows, PAGE_SIZE:PAGE_SIZE + 1]
    suf_scr[...] = jnp.broadcast_to(suf, suf_scr.shape)
    m_old = m_scr[:, 0:1]
    m_new = m_old
    for s in scores:
        m_new = jnp.maximum(m_new, jnp.max(s, axis=-1, keepdims=True))
    alpha = jnp.exp(m_old - m_new)
    l = alpha * l_scr[:, 0:1]
    acc_scr[...] = alpha * acc_scr[...]
    for s, j in zip(scores, reversed(range(npg))):
        p = jnp.exp(s - m_new)
        l = l + jnp.sum(p, axis=-1, keepdims=True)
        acc_scr[...] += _dot_nt(p.astype(BF), v_refs[j][...].astype(BF))
    m_scr[...] = jnp.broadcast_to(m_new, m_scr.shape)
    l_scr[...] = jnp.broadcast_to(l, l_scr.shape)

    @pl.when(c == n_steps - 1)
    def _():
        o_ref[...] = jnp.sum(jnp.where(own, acc_scr[...] / l, 0.0), axis=0, keepdims=True)


def _paged_attention(page_table, qbd, lfn, snew, vnew, cache_k, cache_v, cache_lft):
    n, n_pages = page_table.shape
    npg = PAGES_PER_STEP
    n_steps = n_pages // npg

    def page_of(i, c, pt, j):
        return pt[i, (n_steps - 1 - c) * npg + j]

    per_sample = lambda *shape: pl.BlockSpec((None,) + shape, lambda i, c, pt: (i,) + (0,) * len(shape))
    kv_spec = lambda j: pl.BlockSpec((None, B_W, PAGE_SIZE), lambda i, c, pt: (page_of(i, c, pt, j), 0, 0))
    lf_spec = lambda j: pl.BlockSpec((None, H_B, PAGE_SIZE), lambda i, c, pt: (page_of(i, c, pt, j), 0, 0))
    in_specs = ([per_sample(H_B, B_W), per_sample(H_B, 1), per_sample(H_B, 1), per_sample(1, B_W)]
                + [kv_spec(j) for j in range(npg)] + [kv_spec(j) for j in range(npg)]
                + [lf_spec(j) for j in range(npg)])
    grid_spec = pltpu.PrefetchScalarGridSpec(
        num_scalar_prefetch=1, grid=(n, n_steps), in_specs=in_specs,
        out_specs=per_sample(1, B_W),
        scratch_shapes=[pltpu.VMEM((H_B, LANES), F32), pltpu.VMEM((H_B, LANES), F32),
                        pltpu.VMEM((H_B, B_W), F32), pltpu.VMEM((H_B, LANES), F32)])
    return pl.pallas_call(
        functools.partial(_paged_kernel, n_steps=n_steps),
        grid_spec=grid_spec,
        out_shape=jax.ShapeDtypeStruct((n, 1, B_W), F32),
        compiler_params=_params(("arbitrary", "arbitrary")),
        name="paged_attention",
    )(page_table, qbd, lfn, snew, vnew, *([cache_k] * npg), *([cache_v] * npg), *([cache_lft] * npg))


def _odd_sample_kernel(x_ref, g_ref, w_ref, wg_ref, bg_ref, wp_ref, ps_ref, c_ref, n_ref, m_ref, pool_ref,
                       mix_ref, c_out, n_out, m_out, pool_out, *, pool_div):
    nb = x_ref.shape[0]
    h = _rms(x_ref[...], g_ref[...]).astype(BF)
    q = _round_bf(_dot(h, w_ref[:, 0:C_W]))
    kf = _dot(h, w_ref[:, C_W:2 * C_W]) * (DK_C ** -0.5)
    k = _round_bf(kf)
    v = _round_bf(_dot(h, w_ref[:, 2 * C_W:3 * C_W]))
    og = jax.nn.sigmoid(_dot(h, w_ref[:, 3 * C_W:4 * C_W]))
    p = _dot(h, w_ref[:, 4 * C_W:4 * C_W + P_W])
    gz = _dot(h, wg_ref[...]) + bg_ref[...]
    lf = _log_sigmoid(gz)
    eye = (lax.broadcasted_iota(jnp.int32, (DK_C, DK_C), 0)
           == lax.broadcasted_iota(jnp.int32, (DK_C, DK_C), 1))

    for hd in range(H_C):
        lanes = slice(hd * DK_C, (hd + 1) * DK_C)
        li = gz[:, hd:hd + 1]
        lfh = lf[:, H_C + hd:H_C + hd + 1]
        m_prev = m_ref[:, hd:hd + 1]
        qh = q[:, lanes]
        kh = k[:, lanes]
        vh = v[:, lanes]
        n_old = n_ref[:, lanes]
        m_t = jnp.maximum(lfh + m_prev, li)
        sc = jnp.sum(qh * kh, axis=-1, keepdims=True) * jnp.exp(li - m_t)
        w_inter = jnp.exp(lfh + m_prev - m_t)
        qc = jnp.concatenate(
            [_dot(qh[i:i + 1, :].astype(BF), c_ref[i, hd].astype(BF)) for i in range(nb)], axis=0)
        num = _round_bf(sc) * vh + w_inter * qc
        den = sc + w_inter * jnp.sum(qh * _round_bf(n_old), axis=-1, keepdims=True)
        hout = num / jnp.maximum(jnp.abs(den), jnp.exp(-m_t))
        mix_ref[:, lanes] = (og[:, lanes] * hout).astype(BF)
        a = w_inter
        wg = jnp.exp(li - m_t)
        wgk = wg * kf[:, lanes]
        n_out[:, lanes] = a * n_old + _round_bf(wg) * kh
        m_out[:, hd:hd + 1] = m_t
        for i in range(nb):
            kd = jnp.where(eye, jnp.broadcast_to(wgk[i:i + 1, :], (DK_C, DK_C)), 0.0).astype(BF)
            vrep = jnp.broadcast_to(vh[i:i + 1, :], (DK_C, DK_C)).astype(BF)
            c_out[i, hd] = a[i:i + 1, :] * c_ref[i, hd] + _dot(kd, vrep)

    prev = pool_ref[...]
    row = lax.broadcasted_iota(jnp.int32, prev.shape, 1)
    for gi, win in enumerate(POOL_WINDOWS):
        lanes = slice(gi * POOL_G, (gi + 1) * POOL_G)
        tail = jnp.sum(jnp.where(row >= POOL_PREV - (win - 1), prev, 0.0)[:, :, lanes], axis=1)
        mean = (tail + p[:, lanes]) / pool_div[gi]
        y = _dot((mean - p[:, lanes]).astype(BF), wp_ref[gi]) * ps_ref[:, lanes]
        mix_ref[:, C_W + gi * POOL_G:C_W + (gi + 1) * POOL_G] = y.astype(BF)
    pool_out[:, 0:POOL_PREV - 1, :] = pool_ref[:, 1:POOL_PREV, :]
    for i in range(nb):
        pool_out[i, POOL_PREV - 1:POOL_PREV, :] = p[i:i + 1, :]


def _odd_sample(x, g, w_main, wg, bg, wp, ps, c, n, m, pool, *, pool_div, nb):
    nsmp = x.shape[0]
    rows = lambda w: pl.BlockSpec((nb, w), lambda i: (i, 0))
    return pl.pallas_call(
        functools.partial(_odd_sample_kernel, pool_div=pool_div),
        grid=(nsmp // nb,),
        in_specs=[rows(D_MODEL)] + [_const_spec(a.shape) for a in (g, w_main, wg, bg, wp, ps)]
                 + [pl.BlockSpec((nb, H_C, DK_C, DK_C), lambda i: (i, 0, 0, 0)),
                    rows(C_W), rows(H_C),
                    pl.BlockSpec((nb, POOL_PREV, P_W), lambda i: (i, 0, 0))],
        out_specs=(rows(C_W + P_W),
                   pl.BlockSpec((nb, H_C, DK_C, DK_C), lambda i: (i, 0, 0, 0)),
                   rows(C_W), rows(H_C),
                   pl.BlockSpec((nb, POOL_PREV, P_W), lambda i: (i, 0, 0))),
        out_shape=(jax.ShapeDtypeStruct((nsmp, C_W + P_W), BF),
                   jax.ShapeDtypeStruct((nsmp, H_C, DK_C, DK_C), F32),
                   jax.ShapeDtypeStruct((nsmp, C_W), F32),
                   jax.ShapeDtypeStruct((nsmp, H_C), F32),
                   jax.ShapeDtypeStruct((nsmp, POOL_PREV, P_W), F32)),
        compiler_params=_params(("arbitrary",)),
        name="odd_sample",
    )(x, g, w_main, wg, bg, wp, ps, c, n, m, pool)


def _pad_cols(w, n):
    return jnp.pad(w, ((0, 0), (0, n - w.shape[1])))


def kernel(x_prompt, x_sample, cache_k, cache_v, cache_logf, state_conv_a, state_mlstm_c, state_mlstm_n,
           state_mlstm_m, state_pool, state_ffn_conv, page_table, norm_mix, norm_ffn, norm_final, w_in_even,
           b_forget_even, conv_a, w_out_even, w_in_odd, b_igate_odd, b_fgate_odd, w_pool_odd, pool_scale_odd,
           w_out_odd, w_up, ffn_conv_w, ffn_conv_b, w_down):
    b, s, _ = x_prompt.shape
    nsmp = x_sample.shape[0]
    n_pages = page_table.shape[1]
    assert x_sample.shape[1] == 1 and norm_mix.shape[0] == 2
    tm = min(512, s)
    tq = min(512, s)
    assert s % tm == 0 and n_pages % PAGES_PER_STEP == 0

    xs = x_sample.reshape(nsmp, D_MODEL)
    row = lambda v: v.reshape(1, -1)
    g_final = row(norm_final)

    w = w_in_even[0]
    n_main = 3 * A_W + 3 * B_W
    w_main = w[:, :n_main].astype(BF)
    wf = _pad_cols(w[:, n_main:], LANES).astype(BF)
    wft = jnp.pad(w[:, n_main:].T, ((0, 2 * SUBLANES - H_B), (0, 0))).astype(BF)
    bfl = _pad_cols(row(b_forget_even[0]), LANES)
    bft = jnp.pad(b_forget_even[0].reshape(H_B, 1), ((0, 2 * SUBLANES - H_B), (0, 0)))
    g_mix = row(norm_mix[0])
    wo = w_out_even[0].astype(BF)
    wu_all = w_up.astype(BF)
    wd_all = w_down.astype(BF)
    ffn0 = (wu_all, ffn_conv_w[0], ffn_conv_b[0].reshape(1, D_FF), wd_all)

    wqt = w[:, 3 * A_W:3 * A_W + B_W].T.astype(BF)
    wvt = w[:, 3 * A_W + 2 * B_W:n_main].T.astype(BF)
    a_p, qt_p, k_p, v_p, kb_p, vt_p, lf_p, fc_p, fr_p, ca_p = _even_proj(
        x_prompt, g_mix, w_main, wqt, wvt, wf, wft, bfl, bft, conv_a[0], tm=tm, tk=tq)
    att_p = _fox_prompt(qt_p, kb_p, vt_p, fc_p, fr_p, tq=tq)
    xp, ff0_p = _mix_ffn(x_prompt, a_p, att_p, 0, wo, row(norm_ffn[0]), *ffn0, g_final,
                         layer=0, tm=tm, final_norm=False)

    seg = jnp.asarray((np.arange(B_W)[:, None] // HD_B == np.arange(LANES)[None, :]).astype(np.float32), dtype=BF)
    a_s, q_s, k_s, v_s, lf_s, snew_s, ca_s = _even_sample(
        xs, g_mix, w_main, wf, bfl, conv_a[0], state_conv_a[0].reshape(nsmp, 2 * A_W), seg)
    head_cols = jnp.asarray(np.arange(B_W)[None, :] // HD_B == np.arange(H_B)[:, None])
    qbd = jnp.where(head_cols[None], q_s[:, None, :], 0.0).astype(BF)
    n_phys = cache_k.shape[1]
    kt = cache_k[0].transpose(0, 2, 3, 1).reshape(n_phys, B_W, PAGE_SIZE)
    vt = cache_v[0].transpose(0, 2, 3, 1).reshape(n_phys, B_W, PAGE_SIZE)
    att_s = _paged_attention(
        page_table, qbd, lf_s.reshape(nsmp, H_B, 1), snew_s[:, :H_B].reshape(nsmp, H_B, 1),
        v_s.reshape(nsmp, 1, B_W), kt, vt, cache_logf[0].transpose(0, 2, 1))
    xs1, ff0_s = _mix_ffn_sample(xs, a_s, att_s.reshape(nsmp, B_W).astype(BF), wo, row(norm_ffn[0]), *ffn0,
                                 g_final, state_ffn_conv[0].reshape(nsmp, 2 * D_FF), layer=0, final_norm=False)

    w = w_in_odd[0]
    n_qkvo = 4 * C_W
    w_main = jnp.concatenate([w[:, :n_qkvo], w[:, n_qkvo + 2 * H_C:]], axis=1).astype(BF)
    w_gate = w[:, n_qkvo:n_qkvo + 2 * H_C]
    wg = _pad_cols(w_gate, LANES).astype(BF)
    wgt = jnp.pad(w_gate.T, ((0, 2 * SUBLANES - 2 * H_C), (0, 0))).astype(BF)
    b_gate = jnp.concatenate([b_igate_odd[0], b_fgate_odd[0]])
    bg = _pad_cols(row(b_gate), LANES)
    bgt = jnp.pad(b_gate.reshape(2 * H_C, 1), ((0, 2 * SUBLANES - 2 * H_C), (0, 0)))
    g_mix = row(norm_mix[1])
    wp = w_pool_odd[0].astype(BF)
    ps = row(pool_scale_odd[0])
    wo = w_out_odd[0].astype(BF)
    ffn1 = (wu_all, ffn_conv_w[1], ffn_conv_b[1].reshape(1, D_FF), wd_all)

    mix_p, c_p, n_p, m_p, pool_p = _odd_prompt(xp, g_mix, w_main, wg, wgt, bg, bgt, wp, ps, tm=tm)
    y_p, ff1_p = _mix_ffn(xp, mix_p, mix_p, 1, wo, row(norm_ffn[1]), *ffn1, g_final,
                          layer=1, tm=tm, final_norm=True)

    pos0 = n_pages * PAGE_SIZE
    pool_div = tuple(float(min(win, pos0 + 1)) for win in POOL_WINDOWS)
    mix_s, c_s, n_s, m_s, pool_s = _odd_sample(
        xs1, g_mix, w_main, wg, bg, wp, ps, state_mlstm_c[0], state_mlstm_n[0].reshape(nsmp, C_W),
        state_mlstm_m[0], state_pool[0], pool_div=pool_div, nb=min(8, nsmp))
    y_s, ff1_s = _mix_ffn_sample(xs1, mix_s[:, :C_W], mix_s[:, C_W:], wo, row(norm_ffn[1]), *ffn1, g_final,
                                 state_ffn_conv[1].reshape(nsmp, 2 * D_FF), layer=1, final_norm=True)

    heads = lambda z: z.reshape(z.shape[:-1] + (H_B, HD_B))
    return (
        y_p, y_s.reshape(nsmp, 1, D_MODEL),
        heads(k_p)[None], heads(k_s).reshape(1, nsmp, 1, H_B, HD_B),
        heads(v_p)[None], heads(v_s).reshape(1, nsmp, 1, H_B, HD_B),
        lf_p[None], lf_s.reshape(1, nsmp, 1, H_B),
        ca_p[None], ca_s.reshape(1, nsmp, 2, A_W),
        c_p[None], c_s[None],
        n_p[:, :H_C][None], n_s.reshape(1, nsmp, H_C, DK_C),
        m_p[:, :H_C, 0][None], m_s[None],
        pool_p[:, 1:][None], pool_s[None],
        jnp.stack([ff0_p, ff1_p]),
        jnp.stack([ff0_s.reshape(nsmp, 2, D_FF), ff1_s.reshape(nsmp, 2, D_FF)]),
    )
```
